```python
import jax, jax.numpy as jnp
from jax import lax
import numpy as np

D_MODEL = 2048
BATCH = 2
SEQ = 16384
DEPTH = 2

CTX_LEN = 256
GRID_W = 64
D_MIX = D_MODEL
A_HEADS = 8
A_DK = 128
A_DV = 128
D_A = A_HEADS * A_DV
F_GROUPS = 4
F_DIM = 128
D_F = F_GROUPS * F_DIM
C_GROUPS = 4
C_DIM = 128
D_C = C_GROUPS * C_DIM
MLP_CHUNK = 128
SCAN_CHUNK = 64
D_FF = 5632
CONV_K = 3
D_IN = 5 * D_A + D_F + 2 * D_C
SPLIT_IDX = [D_A, 2 * D_A, 3 * D_A, 4 * D_A, 5 * D_A, 5 * D_A + D_F, 5 * D_A + D_F + D_C]
ALPHA = (2 * DEPTH) ** 0.25
BETA = (8 * DEPTH) ** -0.25
LN_EPS = 1e-6
RMS_EPS = 1e-6

kernel_name = 'hymba_style_hgrn2_fnet_gmlp_convffn_dit'


def layer_norm(x, g=None, b=None):
    xf = x.astype(jnp.float32)
    mu = jnp.mean(xf, -1, keepdims=True)
    var = jnp.mean(jnp.square(xf - mu), -1, keepdims=True)
    y = (xf - mu) * lax.rsqrt(var + LN_EPS)
    if g is not None:
        y = y * g.astype(jnp.float32) + b.astype(jnp.float32)
    return y.astype(x.dtype)


def ada_modulation(cond, w_ada, b_ada):
    m = jax.nn.silu(cond) @ w_ada + b_ada
    return jnp.split(m, 6, axis=-1)


def modulate(x, shift, scale):
    return layer_norm(x) * (1.0 + scale[:, None, :]) + shift[:, None, :]


def post_norm(x, y, g, b):
    return layer_norm(ALPHA * x + y, g, b)


def gated_chunk_scan(q, k, v, logf, s0):
    B_, L, H, _ = q.shape
    V = v.shape[-1]
    n = L // SCAN_CHUNK

    def to_chunks(t):
        return t.reshape(B_, n, SCAN_CHUNK, H, t.shape[-1]).transpose(1, 0, 3, 2, 4)

    incl = jnp.tril(jnp.ones((SCAN_CHUNK, SCAN_CHUNK), bool))[:, :, None]

    def step(S, inp):
        qc, kc, vc, gc = inp
        b = jnp.cumsum(gc, axis=-2)
        o_inter = jnp.einsum('bhck,bhkv->bhcv', qc * jnp.exp(b), S)
        diff = b[:, :, :, None, :] - b[:, :, None, :, :]
        decay = jnp.where(incl, jnp.exp(jnp.where(incl, diff, 0.0)), 0.0)
        scores = jnp.einsum('bhtk,bhtsk,bhsk->bhts', qc, decay, kc)
        o = o_inter + jnp.einsum('bhts,bhsv->bhtv', scores, vc)
        b_last = b[:, :, -1:, :]
        S = jnp.exp(b_last[:, :, 0, :])[..., None] * S + jnp.einsum('bhsk,bhsv->bhkv', kc * jnp.exp(b_last - b), vc)
        return S, o

    S, o = lax.scan(step, s0, (to_chunks(q), to_chunks(k), to_chunks(v), to_chunks(logf)))
    o = o.transpose(1, 0, 3, 2, 4).reshape(B_, L, H, V)
    return o, S


def hgrn2_mixer(zq, zi, zf_fwd, zf_bwd, zg, lb_fwd, lb_bwd, s0_fwd, s0_bwd, norm_g, with_output=True):
    B_, L, _ = zq.shape
    heads = lambda t: t.astype(jnp.float32).reshape(B_, L, A_HEADS, -1)
    q = jax.nn.silu(heads(zq))
    v = heads(zi)

    def gates(zf, lb):
        zf = heads(zf)
        lb = lb.reshape(A_HEADS, A_DK)
        logf = jnp.logaddexp(jax.nn.log_sigmoid(zf) + jnp.log1p(-lb), jnp.log(lb))
        k = (1.0 - lb) * jax.nn.sigmoid(-zf)
        return k, logf

    k_f, logf_f = gates(zf_fwd, lb_fwd)
    k_b, logf_b = gates(zf_bwd, lb_bwd)
    rev = lambda t: jnp.flip(t, axis=1)
    o_f, s_f = gated_chunk_scan(q, k_f, v, logf_f, s0_fwd)
    o_b, s_b = gated_chunk_scan(rev(q), rev(k_b), rev(v), rev(logf_b), s0_bwd)
    if not with_output:
        return None, s_f, s_b
    o = o_f + rev(o_b)
    o = o * lax.rsqrt(jnp.mean(jnp.square(o), -1, keepdims=True) + RMS_EPS) * norm_g.astype(jnp.float32).reshape(A_HEADS, A_DV)
    o = o * jax.nn.silu(heads(zg))
    return o.reshape(B_, L, D_A), s_f, s_b


def fourier_mixer(z):
    B_, L, _ = z.shape
    zf = z.astype(jnp.float32).reshape(B_, L, F_GROUPS, F_DIM)
    y = jnp.fft.fft2(zf, axes=(1, 3), norm='ortho').real
    return y.reshape(B_, L, D_F)


def chunk_mlp_mixer(zu, zv, ln_g, ln_b, w_s, b_s):
    B_, L, _ = zu.shape
    n = L // MLP_CHUNK
    u = jax.nn.gelu(zu)
    v = layer_norm(jax.nn.gelu(zv), ln_g, ln_b).reshape(B_, n, MLP_CHUNK, C_GROUPS, C_DIM)
    v = jnp.einsum('gpq,bnqgc->bnpgc', w_s, v) + b_s.T[None, None, :, :, None]
    return u * v.reshape(B_, L, D_C)


def mixer_merge(z, o_a, sg_g, sg_b, w_s, b_s, w_o):
    dt = z[6].dtype
    y = jnp.concatenate([o_a.astype(dt), fourier_mixer(z[5]).astype(dt),
                         chunk_mlp_mixer(z[6], z[7], sg_g, sg_b, w_s, b_s)], axis=-1)
    return y @ w_o


def conv_ffn(h, w_gate, w_up, conv_w, conv_b, w_down, height):
    B_, L, _ = h.shape
    g = (h @ w_gate).reshape(B_, height, L // height, D_FF)
    g = lax.conv_general_dilated(g, conv_w[:, :, None, :], (1, 1), 'SAME',
                                 dimension_numbers=('NHWC', 'HWIO', 'NHWC'),
                                 feature_group_count=D_FF) + conv_b
    hid = jax.nn.silu(g.reshape(B_, L, D_FF)) * (h @ w_up)
    return hid @ w_down


def setup_inputs(seed: int = 0) -> dict:
    key = jax.random.key(seed)
    ks = jax.random.split(key, 24)
    f32 = jnp.float32
    nrm = lambda k, shape, s: jax.random.normal(k, shape, f32) * s
    return {
        'x': nrm(ks[0], (BATCH, SEQ, D_MODEL), 1.0),
        'c': nrm(ks[1], (BATCH, D_MODEL), 1.0),
        'ctx': nrm(ks[2], (BATCH, CTX_LEN, D_MODEL), 1.0),
        'c_ctx': nrm(ks[3], (D_MODEL,), 1.0),
        'w_ada': nrm(ks[4], (DEPTH, D_MODEL, 6 * D_MODEL), 0.5 * D_MODEL ** -0.5),
        'b_ada': nrm(ks[5], (DEPTH, 6 * D_MODEL), 0.01),
        'w_in': nrm(ks[6], (DEPTH, D_MODEL, D_IN), D_MODEL ** -0.5),
        'lower_bounds': nrm(ks[7], (2, DEPTH, D_A), 0.1),
        'a_norm_g': 1.0 + nrm(ks[8], (DEPTH, D_A), 0.02),
        'sg_norm_g': 1.0 + nrm(ks[9], (DEPTH, D_C), 0.02),
        'sg_norm_b': nrm(ks[10], (DEPTH, D_C), 0.01),
        'w_spatial': nrm(ks[11], (DEPTH, C_GROUPS, MLP_CHUNK, MLP_CHUNK), MLP_CHUNK ** -0.5),
        'b_spatial': 1.0 + nrm(ks[12], (DEPTH, C_GROUPS, MLP_CHUNK), 0.01),
        'w_out': nrm(ks[13], (DEPTH, D_MIX, D_MODEL), BETA * D_MIX ** -0.5),
        'ln1_g': 1.0 + nrm(ks[14], (DEPTH, D_MODEL), 0.02),
        'ln1_b': nrm(ks[15], (DEPTH, D_MODEL), 0.01),
        'w_gate': nrm(ks[16], (DEPTH, D_MODEL, D_FF), D_MODEL ** -0.5),
        'w_up': nrm(ks[17], (DEPTH, D_MODEL, D_FF), D_MODEL ** -0.5),
        'conv_w': nrm(ks[18], (DEPTH, CONV_K, CONV_K, D_FF), 1.0 / CONV_K),
        'conv_b': nrm(ks[19], (DEPTH, D_FF), 0.01),
        'w_down': nrm(ks[20], (DEPTH, D_FF, D_MODEL), BETA * D_FF ** -0.5),
        'ln2_g': 1.0 + nrm(ks[21], (DEPTH, D_MODEL), 0.02),
        'ln2_b': nrm(ks[22], (DEPTH, D_MODEL), 0.01),
    }


def reference(x, c, ctx, c_ctx, w_ada, b_ada, w_in, lower_bounds, a_norm_g, sg_norm_g, sg_norm_b,
              w_spatial, b_spatial, w_out, ln1_g, ln1_b, w_gate, w_up, conv_w, conv_b, w_down, ln2_g, ln2_b):
    B_, L, _ = x.shape
    rows = L // GRID_W
    lbs = jnp.cumsum(jax.nn.softmax(lower_bounds.astype(jnp.float32), axis=1), axis=1)
    lbs = lbs - lbs[:, :1]
    zero_state = jnp.zeros((B_, A_HEADS, A_DK, A_DV), jnp.float32)
    for l in range(DEPTH):
        last = l == DEPTH - 1
        sh1, sc1, g1, sh2, sc2, g2 = ada_modulation(c, w_ada[l], b_ada[l])
        csh1, csc1, cg1, csh2, csc2, cg2 = ada_modulation(c_ctx[None, :], w_ada[l], b_ada[l])

        zc = jnp.split(modulate(ctx, csh1, csc1) @ w_in[l], SPLIT_IDX, axis=-1)
        oa_c, s_f, s_b = hgrn2_mixer(zc[0], zc[1], zc[2], zc[3], zc[4], lbs[0, l], lbs[1, l],
                                     zero_state, zero_state, a_norm_g[l], with_output=not last)

        zx = jnp.split(modulate(x, sh1, sc1) @ w_in[l], SPLIT_IDX, axis=-1)
        oa_x, _, _ = hgrn2_mixer(zx[0], zx[1], zx[2], zx[3], zx[4], lbs[0, l], lbs[1, l],
                                 s_f, s_b, a_norm_g[l])
        y = mixer_merge(zx, oa_x, sg_norm_g[l], sg_norm_b[l], w_spatial[l], b_spatial[l], w_out[l])
        x = post_norm(x, g1[:, None, :] * y, ln1_g[l], ln1_b[l])
        f = conv_ffn(modulate(x, sh2, sc2), w_gate[l], w_up[l], conv_w[l], conv_b[l], w_down[l], rows)
        x = post_norm(x, g2[:, None, :] * f, ln2_g[l], ln2_b[l])

        if not last:
            yc = mixer_merge(zc, oa_c, sg_norm_g[l], sg_norm_b[l], w_spatial[l], b_spatial[l], w_out[l])
            ctx = post_norm(ctx, cg1[:, None, :] * yc, ln1_g[l], ln1_b[l])
            fc = conv_ffn(modulate(ctx, csh2, csc2), w_gate[l], w_up[l], conv_w[l], conv_b[l], w_down[l], 1)
            ctx = post_norm(ctx, cg2[:, None, :] * fc, ln2_g[l], ln2_b[l])
    return x
```

```python
import functools
import math

import numpy as np
import jax
import jax.numpy as jnp
from jax import lax
from jax.experimental import pallas as pl
from jax.experimental.pallas import tpu as pltpu

F32 = jnp.float32
BF16 = jnp.bfloat16

LANES = 128
SUBLANES = 8
VMEM_LIMIT_BYTES = 56 * 1024 * 1024

A_HEADS = 8
F_GROUPS = 4
C_GROUPS = 4
MLP_CHUNK = 128
SCAN_BLOCK = 128
GRID_W = 64
LN_EPS = 1e-6
RMS_EPS = 1e-6


def _cparams(*sem):
    return pltpu.CompilerParams(dimension_semantics=sem, vmem_limit_bytes=VMEM_LIMIT_BYTES)


def _silu(x):
    return x * jax.nn.sigmoid(x)


def _layer_norm(x):
    mu = jnp.mean(x, axis=-1, keepdims=True)
    xc = x - mu
    var = jnp.mean(xc * xc, axis=-1, keepdims=True)
    return xc * lax.rsqrt(var + LN_EPS)


def _cond_spec(arr):
    d = arr.shape[-1]
    if arr.shape[0] == 1:
        return pl.BlockSpec((1, 1, d), lambda b, *_: (0, 0, 0))
    return pl.BlockSpec((1, 1, d), lambda b, *_: (b, 0, 0))


def _ada_kernel(cond_ref, w_ref, b_ref, o_ref):
    s = _silu(cond_ref[...]).astype(BF16)
    o_ref[0] = jnp.dot(s, w_ref[0].astype(BF16), preferred_element_type=F32) + b_ref[0]


def _ada_modulation(cond, w_ada, b_ada):
    depth, d, n = w_ada.shape
    rows = cond.shape[0]
    tn = 1024
    return pl.pallas_call(
        _ada_kernel,
        grid=(depth, n // tn),
        in_specs=[
            pl.BlockSpec((rows, d), lambda l, j: (0, 0)),
            pl.BlockSpec((1, d, tn), lambda l, j: (l, 0, j)),
            pl.BlockSpec((1, 1, tn), lambda l, j: (l, 0, j)),
        ],
        out_specs=pl.BlockSpec((1, rows, tn), lambda l, j: (l, 0, j)),
        out_shape=jax.ShapeDtypeStruct((depth, rows, n), F32),
        compiler_params=_cparams("parallel", "parallel"),
        name="ada_modulation",
    )(cond, w_ada, b_ada.reshape(depth, 1, n))


def _inproj_kernel(x_ref, sh_ref, sc_ref, w_ref, o_ref, h_scr):
    @pl.when(pl.program_id(2) == 0)
    def _():
        h = _layer_norm(x_ref[0]) * (1.0 + sc_ref[0]) + sh_ref[0]
        h_scr[...] = h.astype(BF16)

    o_ref[0] = jnp.dot(h_scr[...], w_ref[...], preferred_element_type=F32)


def _in_projection(x, shift, scale, w):
    b, l, d = x.shape
    n = w.shape[1]
    tm = min(l, 1024)
    tn = 512
    return pl.pallas_call(
        _inproj_kernel,
        grid=(b, l // tm, n // tn),
        in_specs=[
            pl.BlockSpec((1, tm, d), lambda b, i, j: (b, i, 0)),
            _cond_spec(shift),
            _cond_spec(scale),
            pl.BlockSpec((d, tn), lambda b, i, j: (0, j)),
        ],
        out_specs=pl.BlockSpec((1, tm, tn), lambda b, i, j: (b, i, j)),
        out_shape=jax.ShapeDtypeStruct((b, l, n), F32),
        scratch_shapes=[pltpu.VMEM((tm, d), BF16)],
        compiler_params=_cparams("parallel", "parallel", "arbitrary"),
        name="in_projection",
    )(x, shift, scale, w)


def _scan_levels():
    m, out = 1, []
    while m < SCAN_BLOCK:
        out.append(m)
        m *= 2
    return out


def _scan_kernel(*refs, reverse, finalize, n_sub):
    if finalize:
        (zq_ref, zv_ref, zf_ref, lb_ref, s0_ref, zg_ref, oo_ref, ng_ref,
         o_ref, sfin_ref, s_scr, c_scr) = refs
    else:
        zq_ref, zv_ref, zf_ref, lb_ref, s0_ref, o_ref, sfin_ref, s_scr, c_scr = refs
    nb = SCAN_BLOCK
    ci = pl.program_id(2)

    @pl.when(ci == 0)
    def _():
        s_scr[...] = s0_ref[0, 0]

    lb = lb_ref[0]
    log1m_lb = jnp.log1p(-lb)
    log_lb = jnp.log(lb)
    one_m_lb = 1.0 - lb

    row = lax.broadcasted_iota(jnp.int32, (nb, nb), 0)
    col = lax.broadcasted_iota(jnp.int32, (nb, nb), 1)
    tri = jnp.where((col >= row) if reverse else (col <= row), 1.0, 0.0).astype(BF16)
    group_xor = row ^ col
    sub = lax.broadcasted_iota(jnp.int32, (SUBLANES, LANES), 0)

    def ref_rows(m):
        pieces = []
        if m >= SUBLANES:
            for g in range(nb // (2 * m)):
                r = g * 2 * m + (m if reverse else m - 1)
                rb = jnp.broadcast_to(c_scr[r:r + 1, :], (SUBLANES, LANES))
                pieces += [rb] * (2 * m // SUBLANES)
        else:
            for v in range(nb // SUBLANES):
                cur = None
                for g in range(SUBLANES // (2 * m)):
                    r = v * SUBLANES + g * 2 * m + (m if reverse else m - 1)
                    rb = jnp.broadcast_to(c_scr[r:r + 1, :], (SUBLANES, LANES))
                    cur = rb if cur is None else jnp.where(sub >= g * 2 * m, rb, cur)
                pieces.append(cur)
        return jnp.concatenate(pieces, axis=0)

    nt = (((1,), (1,)), ((), ()))
    tn = (((0,), (0,)), ((), ()))

    def block(i, carry):
        j = (n_sub - 1 - i) if reverse else i
        off = pl.multiple_of(j * nb, nb)
        zq = zq_ref[0, pl.ds(off, nb), :]
        v = zv_ref[0, pl.ds(off, nb), :]
        zf = zf_ref[0, pl.ds(off, nb), :]
        q = _silu(zq)

        e = jnp.exp(-jnp.abs(zf))
        inv = 1.0 / (1.0 + e)
        a = jnp.minimum(zf, 0.0) - jnp.log1p(e) + log1m_lb
        logf = jnp.maximum(a, log_lb) + jnp.log1p(jnp.exp(-jnp.abs(a - log_lb)))
        k = one_m_lb * jnp.where(zf > 0.0, e, 1.0) * inv

        g_hi = logf.astype(BF16)
        r1 = logf - g_hi.astype(F32)
        g_mid = r1.astype(BF16)
        g_lo = (r1 - g_mid.astype(F32)).astype(BF16)
        c = (jnp.dot(tri, g_hi, preferred_element_type=F32)
             + jnp.dot(tri, g_mid, preferred_element_type=F32)
             + jnp.dot(tri, g_lo, preferred_element_type=F32))
        c_scr[...] = c

        scores = jnp.where(
            group_xor == 0,
            lax.dot_general(q.astype(BF16), k.astype(BF16), nt, preferred_element_type=F32), 0.0)
        for m in _scan_levels():
            w = jnp.exp(-jnp.abs(c - ref_rows(m)))
            upper = (row & m) == 0
            q_role = upper if reverse else jnp.logical_not(upper)
            p = jnp.where(q_role, q, k) * w
            qt = jnp.where(q_role, p, 0.0).astype(BF16)
            kt = jnp.where(q_role, 0.0, p).astype(BF16)
            lvl = lax.dot_general(qt, kt, nt, preferred_element_type=F32)
            scores = scores + jnp.where(group_xor < 2 * m, lvl, 0.0)

        s_t = s_scr[...]
        q_in = (q * jnp.exp(c)).astype(BF16)
        o = lax.dot_general(q_in, s_t.astype(BF16), nt, preferred_element_type=F32)
        o = o + jnp.dot(scores.astype(BF16), v.astype(BF16), preferred_element_type=F32)

        tot_row = 0 if reverse else nb - 1
        c_tot = c_scr[tot_row:tot_row + 1, :]
        k_dec = (k * jnp.exp(c_tot - c)).astype(BF16)
        s_scr[...] = s_t * jnp.exp(c_tot) + lax.dot_general(
            v.astype(BF16), k_dec, tn, preferred_element_type=F32)

        if finalize:
            o = o + oo_ref[0, pl.ds(off, nb), :]
            ms = jnp.mean(o * o, axis=-1, keepdims=True)
            o = o * lax.rsqrt(ms + RMS_EPS) * ng_ref[0] * _silu(zg_ref[0, pl.ds(off, nb), :])
        o_ref[0, pl.ds(off, nb), :] = o.astype(o_ref.dtype)
        return carry

    lax.fori_loop(0, n_sub, block, 0)

    @pl.when(ci == pl.num_programs(2) - 1)
    def _():
        sfin_ref[0, 0] = s_scr[...]


def _hgrn2_scan(z, col_q, col_v, col_f, lb, s0, *, reverse, final=None):
    b, l, _ = z.shape
    heads, hd = A_HEADS, LANES
    cb = min(l, 1024)
    nc = l // cb
    n_sub = cb // SCAN_BLOCK

    def chunk(i):
        return (nc - 1 - i) if reverse else i

    def zspec(col0):
        return pl.BlockSpec((1, cb, hd), lambda b, h, i: (b, chunk(i), col0 + h))

    head_row = pl.BlockSpec((1, 1, hd), lambda b, h, i: (h, 0, 0))
    state = pl.BlockSpec((1, 1, hd, hd), lambda b, h, i: (b, h, 0, 0))
    in_specs = [zspec(col_q), zspec(col_v), zspec(col_f), head_row, state]
    args = [z, z, z, lb.reshape(heads, 1, hd), s0]
    if final is not None:
        col_g, other, norm_g = final
        in_specs += [zspec(col_g), zspec(0), head_row]
        args += [z, other, norm_g.reshape(heads, 1, hd)]
    out_dtype = BF16 if final is not None else F32
    return pl.pallas_call(
        functools.partial(_scan_kernel, reverse=reverse, finalize=final is not None, n_sub=n_sub),
        grid=(b, heads, nc),
        in_specs=in_specs,
        out_specs=[zspec(0), state],
        out_shape=[jax.ShapeDtypeStruct((b, l, heads * hd), out_dtype),
                   jax.ShapeDtypeStruct((b, heads, hd, hd), F32)],
        scratch_shapes=[pltpu.VMEM((hd, hd), F32), pltpu.VMEM((SCAN_BLOCK, hd), F32)],
        compiler_params=_cparams("parallel", "parallel", "arbitrary"),
        name="hgrn2_scan_bwd" if reverse else "hgrn2_scan_fwd",
    )(*args)


def _dft_cos_sin(n):
    idx = np.arange(n, dtype=np.int64)
    ang = 2.0 * np.pi * ((idx[:, None] * idx[None, :]) % n).astype(np.float64) / n
    return np.cos(ang), np.sin(ang)


def _fourier_two_stage_kernel(z_ref, csc_ref, csq_ref, csr_ref, tw1_ref, o_ref, tre_scr, tim_scr, *, q):
    nr = LANES
    csc = csc_ref[...]
    csq = csq_ref[...]
    tw_c1 = tw1_ref[0]
    tw_s1 = tw1_ref[1]

    def stage_a(r, tw):
        tw_c, tw_s = tw
        x = z_ref[0, pl.ds(r, q, stride=nr), :]
        uv = jnp.dot(x.astype(BF16), csc, preferred_element_type=F32)
        res = jnp.dot(csq, uv.astype(BF16), preferred_element_type=F32)
        cu, cv = res[:q, :LANES], res[:q, LANES:]
        su, sv = res[q:, :LANES], res[q:, LANES:]
        g_re = cu - sv
        g_im = -(cv + su)
        tre_scr[pl.ds(r, q, stride=nr), :] = g_re * tw_c + g_im * tw_s
        tim_scr[pl.ds(r, q, stride=nr), :] = g_im * tw_c - g_re * tw_s
        return tw_c * tw_c1 - tw_s * tw_s1, tw_s * tw_c1 + tw_c * tw_s1

    lax.fori_loop(0, nr, stage_a, (jnp.ones((q, LANES), F32), jnp.zeros((q, LANES), F32)))

    csr = csr_ref[...]

    def stage_b(ka, carry):
        off = pl.multiple_of(ka * nr, nr)
        t = jnp.concatenate([tre_scr[pl.ds(off, nr), :], tim_scr[pl.ds(off, nr), :]], axis=0)
        y = jnp.dot(csr, t.astype(BF16), preferred_element_type=F32)
        o_ref[0, pl.ds(ka, nr, stride=q), :] = y
        return carry

    lax.fori_loop(0, q, stage_b, 0)


def _fourier_direct_kernel(z_ref, csc_ref, csl_ref, o_ref):
    uv = jnp.dot(z_ref[0].astype(BF16), csc_ref[...], preferred_element_type=F32)
    t = jnp.concatenate([uv[:, :LANES], uv[:, LANES:]], axis=0)
    o_ref[0] = jnp.dot(csl_ref[...], t.astype(BF16), preferred_element_type=F32)


def _fourier_mixer(z, col0):
    b, l, _ = z.shape
    scale = 1.0 / math.sqrt(l * LANES)
    cc, sc = _dft_cos_sin(LANES)
    csc = jnp.asarray(np.concatenate([cc, sc], axis=1) * scale, BF16)
    out_shape = jax.ShapeDtypeStruct((b, l, F_GROUPS * LANES), F32)
    slab = pl.BlockSpec((1, l, LANES), lambda b, g: (b, 0, col0 + g))
    out_slab = pl.BlockSpec((1, l, LANES), lambda b, g: (b, 0, g))

    def whole(a):
        return pl.BlockSpec(a.shape, lambda b, g: (0,) * a.ndim)

    if l <= 1024:
        cl, sl = _dft_cos_sin(l)
        csl = jnp.asarray(np.concatenate([cl, -sl], axis=1), BF16)
        return pl.pallas_call(
            _fourier_direct_kernel,
            grid=(b, F_GROUPS),
            in_specs=[slab, whole(csc), whole(csl)],
            out_specs=out_slab,
            out_shape=out_shape,
            compiler_params=_cparams("parallel", "parallel"),
            name="fourier_direct",
        )(z, csc, csl)

    q = l // LANES
    cq, sq = _dft_cos_sin(q)
    csq = jnp.asarray(np.concatenate([cq, sq], axis=0), BF16)
    csr = jnp.asarray(np.concatenate([cc, sc], axis=1), BF16)
    ang1 = 2.0 * np.pi * np.arange(q, dtype=np.float64) / l
    tw1 = jnp.asarray(np.stack([np.broadcast_to(np.cos(ang1)[:, None], (q, LANES)),
                                np.broadcast_to(np.sin(ang1)[:, None], (q, LANES))]), F32)
    return pl.pallas_call(
        functools.partial(_fourier_two_stage_kernel, q=q),
        grid=(b, F_GROUPS),
        in_specs=[pl.BlockSpec((1, l, LANES), lambda b, g: (b, 0, col0 + g), pipeline_mode=pl.Buffered(1)),
                  whole(csc), whole(csq), whole(csr), whole(tw1)],
        out_specs=pl.BlockSpec((1, l, LANES), lambda b, g: (b, 0, g), pipeline_mode=pl.Buffered(1)),
        out_shape=out_shape,
        scratch_shapes=[pltpu.VMEM((l, LANES), F32), pltpu.VMEM((l, LANES), F32)],
        compiler_params=_cparams("parallel", "parallel"),
        name="fourier_two_stage",
    )(z, csc, csq, csr, tw1)


def _gelu(x):
    return jax.nn.gelu(x, approximate=True)


def _chunk_mlp_kernel(zu_ref, zv_ref, g_ref, b_ref, ws_ref, bs_ref, o_ref, *, n_chunks):
    v = _layer_norm(_gelu(zv_ref[0])) * g_ref[...] + b_ref[...]
    vb = v.astype(BF16)
    for c in range(n_chunks):
        rows = slice(c * MLP_CHUNK, (c + 1) * MLP_CHUNK)
        for g in range(C_GROUPS):
            cols = slice(g * LANES, (g + 1) * LANES)
            sv = jnp.dot(ws_ref[g], vb[rows, cols], preferred_element_type=F32) + bs_ref[g]
            o_ref[0, rows, cols] = (_gelu(zu_ref[0, rows, cols]) * sv).astype(o_ref.dtype)


def _chunk_mlp_mixer(z, col_u, col_v, ln_g, ln_b, w_s, b_s):
    b, l, _ = z.shape
    dc = C_GROUPS * LANES
    tm = min(l, 512)
    bs_full = jnp.broadcast_to(b_s[:, :, None], (C_GROUPS, MLP_CHUNK, LANES)).astype(F32)

    def whole(a):
        return pl.BlockSpec(a.shape, lambda b, i: (0,) * a.ndim)

    ws = w_s.astype(BF16)
    g2, b2 = ln_g.reshape(1, dc), ln_b.reshape(1, dc)
    return pl.pallas_call(
        functools.partial(_chunk_mlp_kernel, n_chunks=tm // MLP_CHUNK),
        grid=(b, l // tm),
        in_specs=[pl.BlockSpec((1, tm, dc), lambda b, i: (b, i, col_u)),
                  pl.BlockSpec((1, tm, dc), lambda b, i: (b, i, col_v)),
                  whole(g2), whole(b2), whole(ws), whole(bs_full)],
        out_specs=pl.BlockSpec((1, tm, dc), lambda b, i: (b, i, 0)),
        out_shape=jax.ShapeDtypeStruct((b, l, dc), BF16),
        compiler_params=_cparams("parallel", "parallel"),
        name="chunk_mlp_mixer",
    )(z, z, g2, b2, ws, bs_full)


def _outproj_kernel(oa_ref, yf_ref, yc_ref, wa_ref, wf_ref, wc_ref, x_ref, gate_ref, lng_ref, lnb_ref,
                    sh_ref, sc_ref, x1_ref, h_ref, *, alpha):
    y = jnp.dot(oa_ref[0], wa_ref[...], preferred_element_type=F32)
    y = y + jnp.dot(yf_ref[0].astype(BF16), wf_ref[...], preferred_element_type=F32)
    y = y + jnp.dot(yc_ref[0], wc_ref[...], preferred_element_type=F32)
    x1 = _layer_norm(alpha * x_ref[0] + gate_ref[0] * y) * lng_ref[...] + lnb_ref[...]
    x1_ref[0] = x1
    h_ref[0] = (_layer_norm(x1) * (1.0 + sc_ref[0]) + sh_ref[0]).astype(BF16)


def _out_projection(o_a, y_f, y_c, w_out, x, gate, ln_g, ln_b, shift2, scale2, alpha):
    b, l, d = x.shape
    da, df, dc = o_a.shape[-1], y_f.shape[-1], y_c.shape[-1]
    tm = min(l, 512)
    row = lambda w: pl.BlockSpec((1, tm, w), lambda b, i: (b, i, 0))

    def wspec(rows, blk):
        return pl.BlockSpec((rows, d), lambda b, i: (blk, 0), pipeline_mode=pl.Buffered(1))

    vec = pl.BlockSpec((1, d), lambda b, i: (0, 0))
    assert da % df == 0 and df == dc
    return pl.pallas_call(
        functools.partial(_outproj_kernel, alpha=alpha),
        grid=(b, l // tm),
        in_specs=[row(da), row(df), row(dc),
                  wspec(da, 0), wspec(df, da // df), wspec(dc, da // df + 1),
                  row(d), _cond_spec(gate), vec, vec, _cond_spec(shift2), _cond_spec(scale2)],
        out_specs=[row(d), row(d)],
        out_shape=[jax.ShapeDtypeStruct((b, l, d), F32), jax.ShapeDtypeStruct((b, l, d), BF16)],
        compiler_params=_cparams("parallel", "parallel"),
        name="out_projection",
    )(o_a, y_f, y_c, w_out, w_out, w_out, x, gate, ln_g.reshape(1, d), ln_b.reshape(1, d), shift2, scale2)


def _ffn_up_kernel(*refs, width, vertical, tm):
    if vertical:
        h_ref, hp_ref, hn_ref, wg_ref, wu_ref, cw_ref, cb_ref, o_ref, g_scr = refs
    else:
        h_ref, wg_ref, wu_ref, cw_ref, cb_ref, o_ref, g_scr = refs
    i = pl.program_id(1)
    wg = wg_ref[...]
    h = h_ref[0]
    halo = width if vertical else 0
    g_scr[halo:halo + tm, :] = jnp.dot(h, wg, preferred_element_type=F32)
    if vertical:
        not_first = (i > 0).astype(F32)
        not_last = (i < pl.num_programs(1) - 1).astype(F32)
        g_scr[0:halo, :] = jnp.dot(hp_ref[0], wg, preferred_element_type=F32) * not_first
        g_scr[halo + tm:, :] = jnp.dot(hn_ref[0], wg, preferred_element_type=F32) * not_last
    ext = g_scr[...]
    n_ext = ext.shape[0]
    colpos = lax.broadcasted_iota(jnp.int32, ext.shape, 0) % width
    left = jnp.where(colpos == 0, 0.0, pltpu.roll(ext, 1, axis=0))
    right = jnp.where(colpos == width - 1, 0.0, pltpu.roll(ext, n_ext - 1, axis=0))
    taps = (left, ext, right)
    acc = None
    for dh in (range(3) if vertical else (1,)):
        base = halo + (dh - 1) * width
        for dw in range(3):
            term = taps[dw][base:base + tm, :] * cw_ref[dh * 3 + dw:dh * 3 + dw + 1, :]
            acc = term if acc is None else acc + term
    gate = _silu(acc + cb_ref[...])
    up = jnp.dot(h, wu_ref[...], preferred_element_type=F32)
    o_ref[0] = (gate * up).astype(o_ref.dtype)


def _ffn_up(h, w_gate, w_up, conv_w, conv_b, width):
    b, l, d = h.shape
    f = w_gate.shape[1]
    vertical = l > width
    tm = min(l, 1024)
    tf = 512
    assert tm % width == 0 and f % tf == 0
    rows_per_tile = tm // width
    n_rows = l // width
    in_specs = [pl.BlockSpec((1, tm, d), lambda b, i, j: (b, i, 0))]
    args = [h]
    if vertical:
        in_specs += [
            pl.BlockSpec((1, width, d), lambda b, i, j: (b, jnp.maximum(i * rows_per_tile - 1, 0), 0)),
            pl.BlockSpec((1, width, d), lambda b, i, j: (b, jnp.minimum((i + 1) * rows_per_tile, n_rows - 1), 0)),
        ]
        args += [h, h]
    wspec = pl.BlockSpec((d, tf), lambda b, i, j: (0, j))
    in_specs += [wspec, wspec,
                 pl.BlockSpec((9, tf), lambda b, i, j: (0, j)),
                 pl.BlockSpec((1, tf), lambda b, i, j: (0, j))]
    args += [w_gate, w_up, conv_w.reshape(9, f), conv_b.reshape(1, f)]
    ext_rows = tm + (2 * width if vertical else 0)
    return pl.pallas_call(
        functools.partial(_ffn_up_kernel, width=width, vertical=vertical, tm=tm),
        grid=(b, l // tm, f // tf),
        in_specs=in_specs,
        out_specs=pl.BlockSpec((1, tm, tf), lambda b, i, j: (b, i, j)),
        out_shape=jax.ShapeDtypeStruct((b, l, f), BF16),
        scratch_shapes=[pltpu.VMEM((ext_rows, tf), F32)],
        compiler_params=_cparams("parallel", "parallel", "arbitrary"),
        name="ffn_up",
    )(*args)


def _ffn_down_kernel(hid_ref, w_ref, x_ref, gate_ref, lng_ref, lnb_ref, o_ref, acc_scr, *, alpha):
    k = pl.program_id(2)
    part = jnp.dot(hid_ref[0], w_ref[...], preferred_element_type=F32)

    @pl.when(k == 0)
    def _():
        acc_scr[...] = part

    @pl.when(k > 0)
    def _():
        acc_scr[...] += part

    @pl.when(k == pl.num_programs(2) - 1)
    def _():
        y = alpha * x_ref[0] + gate_ref[0] * acc_scr[...]
        o_ref[0] = _layer_norm(y) * lng_ref[...] + lnb_ref[...]


def _ffn_down(hid, w_down, x, gate, ln_g, ln_b, alpha):
    b, l, f = hid.shape
    d = x.shape[-1]
    tm = min(l, 512)
    tk = 512
    vec = pl.BlockSpec((1, d), lambda b, i, k: (0, 0))
    return pl.pallas_call(
        functools.partial(_ffn_down_kernel, alpha=alpha),
        grid=(b, l // tm, f // tk),
        in_specs=[pl.BlockSpec((1, tm, tk), lambda b, i, k: (b, i, k)),
                  pl.BlockSpec((tk, d), lambda b, i, k: (k, 0)),
                  pl.BlockSpec((1, tm, d), lambda b, i, k: (b, i, 0)),
                  _cond_spec(gate), vec, vec],
        out_specs=pl.BlockSpec((1, tm, d), lambda b, i, k: (b, i, 0)),
        out_shape=jax.ShapeDtypeStruct((b, l, d), F32),
        scratch_shapes=[pltpu.VMEM((tm, d), F32)],
        compiler_params=_cparams("parallel", "parallel", "arbitrary"),
        name="ffn_down",
    )(hid, w_down, x, gate, ln_g.reshape(1, d), ln_b.reshape(1, d))


def kernel(x, c, ctx, c_ctx, w_ada, b_ada, w_in, lower_bounds, a_norm_g, sg_norm_g, sg_norm_b,
           w_spatial, b_spatial, w_out, ln1_g, ln1_b, w_gate, w_up, conv_w, conv_b, w_down, ln2_g, ln2_b):
    batch, seq, d = x.shape
    depth = w_in.shape[0]
    d_a = a_norm_g.shape[-1]
    d_c = sg_norm_g.shape[-1]
    d_f = w_in.shape[-1] - 5 * d_a - 2 * d_c
    assert d_a == A_HEADS * LANES and d_c == C_GROUPS * LANES and d_f == F_GROUPS * LANES
    alpha = (2 * depth) ** 0.25
    a_blk = d_a // LANES

    lbs = jnp.cumsum(jax.nn.softmax(lower_bounds.astype(F32), axis=1), axis=1)
    lbs = lbs - lbs[:, :1]

    rows = -(-(batch + 1) // SUBLANES) * SUBLANES
    cond = jnp.zeros((rows, d), F32).at[:batch].set(c).at[batch].set(c_ctx)
    mods = _ada_modulation(cond, w_ada, b_ada)

    w_in_b, w_out_b = w_in.astype(BF16), w_out.astype(BF16)
    w_gate_b, w_up_b, w_down_b = w_gate.astype(BF16), w_up.astype(BF16), w_down.astype(BF16)
    zero_state = jnp.zeros((batch, A_HEADS, LANES, LANES), F32)

    def mixers(z, s_fwd, s_bwd, l, with_output=True):
        o_b, sf_b = _hgrn2_scan(z, 0, a_blk, 3 * a_blk, lbs[1, l], s_bwd, reverse=True)
        if not with_output:
            _, sf_f = _hgrn2_scan(z, 0, a_blk, 2 * a_blk, lbs[0, l], s_fwd, reverse=False)
            return None, sf_f, sf_b
        o_a, sf_f = _hgrn2_scan(z, 0, a_blk, 2 * a_blk, lbs[0, l], s_fwd, reverse=False,
                                final=(4 * a_blk, o_b, a_norm_g[l]))
        y_f = _fourier_mixer(z, 5 * a_blk)
        cu = (5 * d_a + d_f) // d_c
        y_c = _chunk_mlp_mixer(z, cu, cu + 1, sg_norm_g[l], sg_norm_b[l], w_spatial[l], b_spatial[l])
        return (o_a, y_f, y_c), sf_f, sf_b

    def rest_of_layer(xs, ys, m, l, width):
        sh2, sc2 = m[3], m[4]
        x1, h2 = _out_projection(*ys, w_out_b[l], xs, m[2], ln1_g[l], ln1_b[l], sh2, sc2, alpha)
        hid = _ffn_up(h2, w_gate_b[l], w_up_b[l], conv_w[l], conv_b[l], width)
        return _ffn_down(hid, w_down_b[l], x1, m[5], ln2_g[l], ln2_b[l], alpha)

    for l in range(depth):
        last = l == depth - 1
        m_all = [mods[l, :, i * d:(i + 1) * d] for i in range(6)]
        m_x = [t[:batch].reshape(batch, 1, d) for t in m_all]
        m_c = [t[batch:batch + 1].reshape(1, 1, d) for t in m_all]

        zc = _in_projection(ctx, m_c[0], m_c[1], w_in_b[l])
        ys_c, s_f, s_b = mixers(zc, zero_state, zero_state, l, with_output=not last)
        zx = _in_projection(x, m_x[0], m_x[1], w_in_b[l])
        ys_x, _, _ = mixers(zx, s_f, s_b, l)
        x = rest_of_layer(x, ys_x, m_x, l, GRID_W)
        if not last:
            ctx = rest_of_layer(ctx, ys_c, m_c, l, ctx.shape[1])
    return x
```

```python
import functools
import math

import numpy as np
import jax
import jax.numpy as jnp
from jax import lax
from jax.experimental import pallas as pl
from jax.experimental.pallas import tpu as pltpu

F32 = jnp.float32
BF16 = jnp.bfloat16

LANES = 128
SUBLANES = 8
VMEM_LIMIT_BYTES = 56 * 1024 * 1024

A_HEADS = 8
F_GROUPS = 4
C_GROUPS = 4
MLP_CHUNK = 128
SCAN_BLOCK = 128
GRID_W = 64
LN_EPS = 1e-6
RMS_EPS = 1e-6


def _cparams(*sem):
    return pltpu.CompilerParams(dimension_semantics=sem, vmem_limit_bytes=VMEM_LIMIT_BYTES)


def _silu(x):
    return x * jax.nn.sigmoid(x)


def _layer_norm(x):
    mu = jnp.mean(x, axis=-1, keepdims=True)
    xc = x - mu
    var = jnp.mean(xc * xc, axis=-1, keepdims=True)
    return xc * lax.rsqrt(var + LN_EPS)


def _cond_spec(arr):
    d = arr.shape[-1]
    if arr.shape[0] == 1:
        return pl.BlockSpec((1, 1, d), lambda b, *_: (0, 0, 0))
    return pl.BlockSpec((1, 1, d), lambda b, *_: (b, 0, 0))


def _ada_kernel(cond_ref, w_ref, b_ref, o_ref):
    s = _silu(cond_ref[...]).astype(BF16)
    o_ref[0] = jnp.dot(s, w_ref[0].astype(BF16), preferred_element_type=F32) + b_ref[0]


def _ada_modulation(cond, w_ada, b_ada):
    depth, d, n = w_ada.shape
    rows = cond.shape[0]
    tn = 1024
    return pl.pallas_call(
        _ada_kernel,
        grid=(depth, n // tn),
        in_specs=[
            pl.BlockSpec((rows, d), lambda l, j: (0, 0)),
            pl.BlockSpec((1, d, tn), lambda l, j: (l, 0, j)),
            pl.BlockSpec((1, 1, tn), lambda l, j: (l, 0, j)),
        ],
        out_specs=pl.BlockSpec((1, rows, tn), lambda l, j: (l, 0, j)),
        out_shape=jax.ShapeDtypeStruct((depth, rows, n), F32),
        compiler_params=_cparams("parallel", "parallel"),
        name="ada_modulation",
    )(cond, w_ada, b_ada.reshape(depth, 1, n))


def _inproj_kernel(x_ref, sh_ref, sc_ref, w_ref, o_ref, h_scr):
    @pl.when(pl.program_id(2) == 0)
    def _():
        h = _layer_norm(x_ref[0]) * (1.0 + sc_ref[0]) + sh_ref[0]
        h_scr[...] = h.astype(BF16)

    o_ref[0] = jnp.dot(h_scr[...], w_ref[...], preferred_element_type=F32)


def _in_projection(x, shift, scale, w):
    b, l, d = x.shape
    n = w.shape[1]
    tm = min(l, 1024)
    tn = 512
    return pl.pallas_call(
        _inproj_kernel,
        grid=(b, l // tm, n // tn),
        in_specs=[
            pl.BlockSpec((1, tm, d), lambda b, i, j: (b, i, 0)),
            _cond_spec(shift),
            _cond_spec(scale),
            pl.BlockSpec((d, tn), lambda b, i, j: (0, j)),
        ],
        out_specs=pl.BlockSpec((1, tm, tn), lambda b, i, j: (b, i, j)),
        out_shape=jax.ShapeDtypeStruct((b, l, n), F32),
        scratch_shapes=[pltpu.VMEM((tm, d), BF16)],
        compiler_params=_cparams("parallel", "parallel", "arbitrary"),
        name="in_projection",
    )(x, shift, scale, w)


def _scan_levels():
    m, out = 1, []
    while m < SCAN_BLOCK:
        out.append(m)
        m *= 2
    return out


def _scan_kernel(*refs, reverse, finalize, n_sub):
    if finalize:
        (zq_ref, zv_ref, zf_ref, lb_ref, s0_ref, zg_ref, oo_ref, ng_ref,
         o_ref, sfin_ref, s_scr, c_scr) = refs
    else:
        zq_ref, zv_ref, zf_ref, lb_ref, s0_ref, o_ref, sfin_ref, s_scr, c_scr = refs
    nb = SCAN_BLOCK
    ci = pl.program_id(2)

    @pl.when(ci == 0)
    def _():
        s_scr[...] = s0_ref[0, 0]

    lb = lb_ref[0]
    log1m_lb = jnp.log1p(-lb)
    log_lb = jnp.log(lb)
    one_m_lb = 1.0 - lb

    row = lax.broadcasted_iota(jnp.int32, (nb, nb), 0)
    col = lax.broadcasted_iota(jnp.int32, (nb, nb), 1)
    tri = jnp.where((col >= row) if reverse else (col <= row), 1.0, 0.0).astype(BF16)
    group_xor = row ^ col
    sub = lax.broadcasted_iota(jnp.int32, (SUBLANES, LANES), 0)

    def ref_rows(m):
        pieces = []
        if m >= SUBLANES:
            for g in range(nb // (2 * m)):
                r = g * 2 * m + (m if reverse else m - 1)
                rb = jnp.broadcast_to(c_scr[r:r + 1, :], (SUBLANES, LANES))
                pieces += [rb] * (2 * m // SUBLANES)
        else:
            for v in range(nb // SUBLANES):
                cur = None
                for g in range(SUBLANES // (2 * m)):
                    r = v * SUBLANES + g * 2 * m + (m if reverse else m - 1)
                    rb = jnp.broadcast_to(c_scr[r:r + 1, :], (SUBLANES, LANES))
                    cur = rb if cur is None else jnp.where(sub >= g * 2 * m, rb, cur)
                pieces.append(cur)
        return jnp.concatenate(pieces, axis=0)

    nt = (((1,), (1,)), ((), ()))
    tn = (((0,), (0,)), ((), ()))

    def block(i, carry):
        j = (n_sub - 1 - i) if reverse else i
        off = pl.multiple_of(j * nb, nb)
        zq = zq_ref[0, pl.ds(off, nb), :]
        v = zv_ref[0, pl.ds(off, nb), :]
        zf = zf_ref[0, pl.ds(off, nb), :]
        q = _silu(zq)

        e = jnp.exp(-jnp.abs(zf))
        inv = 1.0 / (1.0 + e)
        a = jnp.minimum(zf, 0.0) - jnp.log1p(e) + log1m_lb
        logf = jnp.maximum(a, log_lb) + jnp.log1p(jnp.exp(-jnp.abs(a - log_lb)))
        k = one_m_lb * jnp.where(zf > 0.0, e, 1.0) * inv

        g_hi = logf.astype(BF16)
        r1 = logf - g_hi.astype(F32)
        g_mid = r1.astype(BF16)
        g_lo = (r1 - g_mid.astype(F32)).astype(BF16)
        c = (jnp.dot(tri, g_hi, preferred_element_type=F32)
             + jnp.dot(tri, g_mid, preferred_element_type=F32)
             + jnp.dot(tri, g_lo, preferred_element_type=F32))
        c_scr[...] = c

        scores = jnp.where(
            group_xor == 0,
            lax.dot_general(q.astype(BF16), k.astype(BF16), nt, preferred_element_type=F32), 0.0)
        for m in _scan_levels():
            w = jnp.exp(-jnp.abs(c - ref_rows(m)))
            upper = (row & m) == 0
            q_role = upper if reverse else jnp.logical_not(upper)
            p = jnp.where(q_role, q, k) * w
            qt = jnp.where(q_role, p, 0.0).astype(BF16)
            kt = jnp.where(q_role, 0.0, p).astype(BF16)
            lvl = lax.dot_general(qt, kt, nt, preferred_element_type=F32)
            scores = scores + jnp.where(group_xor < 2 * m, lvl, 0.0)

        s_t = s_scr[...]
        q_in = (q * jnp.exp(c)).astype(BF16)
        o = lax.dot_general(q_in, s_t.astype(BF16), nt, preferred_element_type=F32)
        o = o + jnp.dot(scores.astype(BF16), v.astype(BF16), preferred_element_type=F32)

        tot_row = 0 if reverse else nb - 1
        c_tot = c_scr[tot_row:tot_row + 1, :]
        k_dec = (k * jnp.exp(c_tot - c)).astype(BF16)
        s_scr[...] = s_t * jnp.exp(c_tot) + lax.dot_general(
            v.astype(BF16), k_dec, tn, preferred_element_type=F32)

        if finalize:
            o = o + oo_ref[0, pl.ds(off, nb), :]
            ms = jnp.mean(o * o, axis=-1, keepdims=True)
            o = o * lax.rsqrt(ms + RMS_EPS) * ng_ref[0] * _silu(zg_ref[0, pl.ds(off, nb), :])
        o_ref[0, pl.ds(off, nb), :] = o.astype(o_ref.dtype)
        return carry

    lax.fori_loop(0, n_sub, block, 0)

    @pl.when(ci == pl.num_programs(2) - 1)
    def _():
        sfin_ref[0, 0] = s_scr[...]


def _hgrn2_scan(z, col_q, col_v, col_f, lb, s0, *, reverse, final=None):
    b, l, _ = z.shape
    heads, hd = A_HEADS, LANES
    cb = min(l, 1024)
    nc = l // cb
    n_sub = cb // SCAN_BLOCK

    def chunk(i):
        return (nc - 1 - i) if reverse else i

    def zspec(col0):
        return pl.BlockSpec((1, cb, hd), lambda b, h, i: (b, chunk(i), col0 + h))

    head_row = pl.BlockSpec((1, 1, hd), lambda b, h, i: (h, 0, 0))
    state = pl.BlockSpec((1, 1, hd, hd), lambda b, h, i: (b, h, 0, 0))
    in_specs = [zspec(col_q), zspec(col_v), zspec(col_f), head_row, state]
    args = [z, z, z, lb.reshape(heads, 1, hd), s0]
    if final is not None:
        col_g, other, norm_g = final
        in_specs += [zspec(col_g), zspec(0), head_row]
        args += [z, other, norm_g.reshape(heads, 1, hd)]
    out_dtype = BF16 if final is not None else F32
    return pl.pallas_call(
        functools.partial(_scan_kernel, reverse=reverse, finalize=final is not None, n_sub=n_sub),
        grid=(b, heads, nc),
        in_specs=in_specs,
        out_specs=[zspec(0), state],
        out_shape=[jax.ShapeDtypeStruct((b, l, heads * hd), out_dtype),
                   jax.ShapeDtypeStruct((b, heads, hd, hd), F32)],
        scratch_shapes=[pltpu.VMEM((hd, hd), F32), pltpu.VMEM((SCAN_BLOCK, hd), F32)],
        compiler_params=_cparams("parallel", "parallel", "arbitrary"),
        name="hgrn2_scan_bwd" if reverse else "hgrn2_scan_fwd",
    )(*args)


def _dft_cos_sin(n):
    idx = np.arange(n, dtype=np.int64)
    ang = 2.0 * np.pi * ((idx[:, None] * idx[None, :]) % n).astype(np.float64) / n
    return np.cos(ang), np.sin(ang)


def _fourier_two_stage_kernel(z_ref, csc_ref, csq_ref, csr_ref, tw1_ref, o_ref, tre_scr, tim_scr, *, q):
    nr = LANES
    csc = csc_ref[...]
    csq = csq_ref[...]
    tw_c1 = tw1_ref[0]
    tw_s1 = tw1_ref[1]

    def stage_a(r, tw):
        tw_c, tw_s = tw
        x = z_ref[0, pl.ds(r, q, stride=nr), :]
        uv = jnp.dot(x.astype(BF16), csc, preferred_element_type=F32)
        res = jnp.dot(csq, uv.astype(BF16), preferred_element_type=F32)
        cu, cv = res[:q, :LANES], res[:q, LANES:]
        su, sv = res[q:, :LANES], res[q:, LANES:]
        g_re = cu - sv
        g_im = -(cv + su)
        tre_scr[pl.ds(r, q, stride=nr), :] = g_re * tw_c + g_im * tw_s
        tim_scr[pl.ds(r, q, stride=nr), :] = g_im * tw_c - g_re * tw_s
        return tw_c * tw_c1 - tw_s * tw_s1, tw_s * tw_c1 + tw_c * tw_s1

    lax.fori_loop(0, nr, stage_a, (jnp.ones((q, LANES), F32), jnp.zeros((q, LANES), F32)), unroll=4)

    csr = csr_ref[...]

    def stage_b(ka, carry):
        off = pl.multiple_of(ka * nr, nr)
        t = jnp.concatenate([tre_scr[pl.ds(off, nr), :], tim_scr[pl.ds(off, nr), :]], axis=0)
        y = jnp.dot(csr, t.astype(BF16), preferred_element_type=F32)
        o_ref[0, pl.ds(ka, nr, stride=q), :] = y
        return carry

    lax.fori_loop(0, q, stage_b, 0, unroll=4)


def _fourier_direct_kernel(z_ref, csc_ref, csl_ref, o_ref):
    uv = jnp.dot(z_ref[0].astype(BF16), csc_ref[...], preferred_element_type=F32)
    t = jnp.concatenate([uv[:, :LANES], uv[:, LANES:]], axis=0)
    o_ref[0] = jnp.dot(csl_ref[...], t.astype(BF16), preferred_element_type=F32)


def _fourier_mixer(z, col0):
    b, l, _ = z.shape
    scale = 1.0 / math.sqrt(l * LANES)
    cc, sc = _dft_cos_sin(LANES)
    csc = jnp.asarray(np.concatenate([cc, sc], axis=1) * scale, BF16)
    out_shape = jax.ShapeDtypeStruct((b, l, F_GROUPS * LANES), F32)
    slab = pl.BlockSpec((1, l, LANES), lambda b, g: (b, 0, col0 + g))
    out_slab = pl.BlockSpec((1, l, LANES), lambda b, g: (b, 0, g))

    def whole(a):
        return pl.BlockSpec(a.shape, lambda b, g: (0,) * a.ndim)

    if l <= 1024:
        cl, sl = _dft_cos_sin(l)
        csl = jnp.asarray(np.concatenate([cl, -sl], axis=1), BF16)
        return pl.pallas_call(
            _fourier_direct_kernel,
            grid=(b, F_GROUPS),
            in_specs=[slab, whole(csc), whole(csl)],
            out_specs=out_slab,
            out_shape=out_shape,
            compiler_params=_cparams("parallel", "parallel"),
            name="fourier_direct",
        )(z, csc, csl)

    q = l // LANES
    cq, sq = _dft_cos_sin(q)
    csq = jnp.asarray(np.concatenate([cq, sq], axis=0), BF16)
    csr = jnp.asarray(np.concatenate([cc, sc], axis=1), BF16)
    ang1 = 2.0 * np.pi * np.arange(q, dtype=np.float64) / l
    tw1 = jnp.asarray(np.stack([np.broadcast_to(np.cos(ang1)[:, None], (q, LANES)),
                                np.broadcast_to(np.sin(ang1)[:, None], (q, LANES))]), F32)
    return pl.pallas_call(
        functools.partial(_fourier_two_stage_kernel, q=q),
        grid=(b, F_GROUPS),
        in_specs=[pl.BlockSpec((1, l, LANES), lambda b, g: (b, 0, col0 + g), pipeline_mode=pl.Buffered(1)),
                  whole(csc), whole(csq), whole(csr), whole(tw1)],
        out_specs=pl.BlockSpec((1, l, LANES), lambda b, g: (b, 0, g), pipeline_mode=pl.Buffered(1)),
        out_shape=out_shape,
        scratch_shapes=[pltpu.VMEM((l, LANES), F32), pltpu.VMEM((l, LANES), F32)],
        compiler_params=_cparams("parallel", "parallel"),
        name="fourier_two_stage",
    )(z, csc, csq, csr, tw1)


def _gelu(x):
    return jax.nn.gelu(x, approximate=True)


def _chunk_mlp_kernel(zu_ref, zv_ref, g_ref, b_ref, ws_ref, bs_ref, o_ref, *, n_chunks):
    v = _layer_norm(_gelu(zv_ref[0])) * g_ref[...] + b_ref[...]
    vb = v.astype(BF16)
    for c in range(n_chunks):
        rows = slice(c * MLP_CHUNK, (c + 1) * MLP_CHUNK)
        for g in range(C_GROUPS):
            cols = slice(g * LANES, (g + 1) * LANES)
            sv = jnp.dot(ws_ref[g], vb[rows, cols], preferred_element_type=F32) + bs_ref[g]
            o_ref[0, rows, cols] = (_gelu(zu_ref[0, rows, cols]) * sv).astype(o_ref.dtype)


def _chunk_mlp_mixer(z, col_u, col_v, ln_g, ln_b, w_s, b_s):
    b, l, _ = z.shape
    dc = C_GROUPS * LANES
    tm = min(l, 512)
    bs_full = jnp.broadcast_to(b_s[:, :, None], (C_GROUPS, MLP_CHUNK, LANES)).astype(F32)

    def whole(a):
        return pl.BlockSpec(a.shape, lambda b, i: (0,) * a.ndim)

    ws = w_s.astype(BF16)
    g2, b2 = ln_g.reshape(1, dc), ln_b.reshape(1, dc)
    return pl.pallas_call(
        functools.partial(_chunk_mlp_kernel, n_chunks=tm // MLP_CHUNK),
        grid=(b, l // tm),
        in_specs=[pl.BlockSpec((1, tm, dc), lambda b, i: (b, i, col_u)),
                  pl.BlockSpec((1, tm, dc), lambda b, i: (b, i, col_v)),
                  whole(g2), whole(b2), whole(ws), whole(bs_full)],
        out_specs=pl.BlockSpec((1, tm, dc), lambda b, i: (b, i, 0)),
        out_shape=jax.ShapeDtypeStruct((b, l, dc), BF16),
        compiler_params=_cparams("parallel", "parallel"),
        name="chunk_mlp_mixer",
    )(z, z, g2, b2, ws, bs_full)


def _outproj_kernel(oa_ref, yf_ref, yc_ref, wa_ref, wf_ref, wc_ref, x_ref, gate_ref, lng_ref, lnb_ref,
                    sh_ref, sc_ref, x1_ref, h_ref, *, alpha):
    y = jnp.dot(oa_ref[0], wa_ref[...], preferred_element_type=F32)
    y = y + jnp.dot(yf_ref[0].astype(BF16), wf_ref[...], preferred_element_type=F32)
    y = y + jnp.dot(yc_ref[0], wc_ref[...], preferred_element_type=F32)
    x1 = _layer_norm(alpha * x_ref[0] + gate_ref[0] * y) * lng_ref[...] + lnb_ref[...]
    x1_ref[0] = x1
    h_ref[0] = (_layer_norm(x1) * (1.0 + sc_ref[0]) + sh_ref[0]).astype(BF16)


def _out_projection(o_a, y_f, y_c, w_out, x, gate, ln_g, ln_b, shift2, scale2, alpha):
    b, l, d = x.shape
    da, df, dc = o_a.shape[-1], y_f.shape[-1], y_c.shape[-1]
    tm = min(l, 512)
    row = lambda w: pl.BlockSpec((1, tm, w), lambda b, i: (b, i, 0))

    def wspec(rows, blk):
        return pl.BlockSpec((rows, d), lambda b, i: (blk, 0), pipeline_mode=pl.Buffered(1))

    vec = pl.BlockSpec((1, d), lambda b, i: (0, 0))
    assert da % df == 0 and df == dc
    return pl.pallas_call(
        functools.partial(_outproj_kernel, alpha=alpha),
        grid=(b, l // tm),
        in_specs=[row(da), row(df), row(dc),
                  wspec(da, 0), wspec(df, da // df), wspec(dc, da // df + 1),
                  row(d), _cond_spec(gate), vec, vec, _cond_spec(shift2), _cond_spec(scale2)],
        out_specs=[row(d), row(d)],
        out_shape=[jax.ShapeDtypeStruct((b, l, d), F32), jax.ShapeDtypeStruct((b, l, d), BF16)],
        compiler_params=_cparams("parallel", "parallel"),
        name="out_projection",
    )(o_a, y_f, y_c, w_out, w_out, w_out, x, gate, ln_g.reshape(1, d), ln_b.reshape(1, d), shift2, scale2)


def _ffn_up_kernel(*refs, width, vertical, tm):
    if vertical:
        h_ref, hp_ref, hn_ref, wg_ref, wu_ref, cw_ref, cb_ref, o_ref, g_scr = refs
    else:
        h_ref, wg_ref, wu_ref, cw_ref, cb_ref, o_ref, g_scr = refs
    i = pl.program_id(1)
    wg = wg_ref[...]
    h = h_ref[0]
    halo = width if vertical else 0
    g_scr[halo:halo + tm, :] = jnp.dot(h, wg, preferred_element_type=F32)
    if vertical:
        not_first = (i > 0).astype(F32)
        not_last = (i < pl.num_programs(1) - 1).astype(F32)
        g_scr[0:halo, :] = jnp.dot(hp_ref[0], wg, preferred_element_type=F32) * not_first
        g_scr[halo + tm:, :] = jnp.dot(hn_ref[0], wg, preferred_element_type=F32) * not_last
    ext = g_scr[...]
    n_ext = ext.shape[0]
    colpos = lax.broadcasted_iota(jnp.int32, ext.shape, 0) % width
    left = jnp.where(colpos == 0, 0.0, pltpu.roll(ext, 1, axis=0))
    right = jnp.where(colpos == width - 1, 0.0, pltpu.roll(ext, n_ext - 1, axis=0))
    taps = (left, ext, right)
    acc = None
    for dh in (range(3) if vertical else (1,)):
        base = halo + (dh - 1) * width
        for dw in range(3):
            term = taps[dw][base:base + tm, :] * cw_ref[dh * 3 + dw:dh * 3 + dw + 1, :]
            acc = term if acc is None else acc + term
    gate = _silu(acc + cb_ref[...])
    up = jnp.dot(h, wu_ref[...], preferred_element_type=F32)
    o_ref[0] = (gate * up).astype(o_ref.dtype)


def _ffn_up(h, w_gate, w_up, conv_w, conv_b, width):
    b, l, d = h.shape
    f = w_gate.shape[1]
    vertical = l > width
    tm = min(l, 1024)
    tf = 512
    assert tm % width == 0 and f % tf == 0
    rows_per_tile = tm // width
    n_rows = l // width
    in_specs = [pl.BlockSpec((1, tm, d), lambda b, i, j: (b, i, 0))]
    args = [h]
    if vertical:
        in_specs += [
            pl.BlockSpec((1, width, d), lambda b, i, j: (b, jnp.maximum(i * rows_per_tile - 1, 0), 0)),
            pl.BlockSpec((1, width, d), lambda b, i, j: (b, jnp.minimum((i + 1) * rows_per_tile, n_rows - 1), 0)),
        ]
        args += [h, h]
    wspec = pl.BlockSpec((d, tf), lambda b, i, j: (0, j))
    in_specs += [wspec, wspec,
                 pl.BlockSpec((9, tf), lambda b, i, j: (0, j)),
                 pl.BlockSpec((1, tf), lambda b, i, j: (0, j))]
    args += [w_gate, w_up, conv_w.reshape(9, f), conv_b.reshape(1, f)]
    ext_rows = tm + (2 * width if vertical else 0)
    return pl.pallas_call(
        functools.partial(_ffn_up_kernel, width=width, vertical=vertical, tm=tm),
        grid=(b, l // tm, f // tf),
        in_specs=in_specs,
        out_specs=pl.BlockSpec((1, tm, tf), lambda b, i, j: (b, i, j)),
        out_shape=jax.ShapeDtypeStruct((b, l, f), BF16),
        scratch_shapes=[pltpu.VMEM((ext_rows, tf), F32)],
        compiler_params=_cparams("parallel", "parallel", "arbitrary"),
        name="ffn_up",
    )(*args)


def _ffn_down_kernel(hid_ref, w_ref, x_ref, gate_ref, lng_ref, lnb_ref, o_ref, *, alpha):
    f = jnp.dot(hid_ref[0], w_ref[...], preferred_element_type=F32)
    y = alpha * x_ref[0] + gate_ref[0] * f
    o_ref[0] = _layer_norm(y) * lng_ref[...] + lnb_ref[...]


def _ffn_down(hid, w_down, x, gate, ln_g, ln_b, alpha):
    b, l, f = hid.shape
    d = x.shape[-1]
    tm = min(l, 256)
    vec = pl.BlockSpec((1, d), lambda b, i: (0, 0))
    return pl.pallas_call(
        functools.partial(_ffn_down_kernel, alpha=alpha),
        grid=(b, l // tm),
        in_specs=[pl.BlockSpec((1, tm, f), lambda b, i: (b, i, 0)),
                  pl.BlockSpec((f, d), lambda b, i: (0, 0), pipeline_mode=pl.Buffered(1)),
                  pl.BlockSpec((1, tm, d), lambda b, i: (b, i, 0)),
                  _cond_spec(gate), vec, vec],
        out_specs=pl.BlockSpec((1, tm, d), lambda b, i: (b, i, 0)),
        out_shape=jax.ShapeDtypeStruct((b, l, d), F32),
        compiler_params=_cparams("parallel", "parallel"),
        name="ffn_down",
    )(hid, w_down, x, gate, ln_g.reshape(1, d), ln_b.reshape(1, d))


def kernel(x, c, ctx, c_ctx, w_ada, b_ada, w_in, lower_bounds, a_norm_g, sg_norm_g, sg_norm_b,
           w_spatial, b_spatial, w_out, ln1_g, ln1_b, w_gate, w_up, conv_w, conv_b, w_down, ln2_g, ln2_b):
    batch, seq, d = x.shape
    depth = w_in.shape[0]
    d_a = a_norm_g.shape[-1]
    d_c = sg_norm_g.shape[-1]
    d_f = w_in.shape[-1] - 5 * d_a - 2 * d_c
    assert d_a == A_HEADS * LANES and d_c == C_GROUPS * LANES and d_f == F_GROUPS * LANES
    alpha = (2 * depth) ** 0.25
    a_blk = d_a // LANES

    lbs = jnp.cumsum(jax.nn.softmax(lower_bounds.astype(F32), axis=1), axis=1)
    lbs = lbs - lbs[:, :1]

    rows = -(-(batch + 1) // SUBLANES) * SUBLANES
    cond = jnp.zeros((rows, d), F32).at[:batch].set(c).at[batch].set(c_ctx)
    mods = _ada_modulation(cond, w_ada, b_ada)

    w_in_b, w_out_b = w_in.astype(BF16), w_out.astype(BF16)
    w_gate_b, w_up_b, w_down_b = w_gate.astype(BF16), w_up.astype(BF16), w_down.astype(BF16)
    zero_state = jnp.zeros((batch, A_HEADS, LANES, LANES), F32)

    def mixers(z, s_fwd, s_bwd, l, with_output=True):
        o_b, sf_b = _hgrn2_scan(z, 0, a_blk, 3 * a_blk, lbs[1, l], s_bwd, reverse=True)
        if not with_output:
            _, sf_f = _hgrn2_scan(z, 0, a_blk, 2 * a_blk, lbs[0, l], s_fwd, reverse=False)
            return None, sf_f, sf_b
        o_a, sf_f = _hgrn2_scan(z, 0, a_blk, 2 * a_blk, lbs[0, l], s_fwd, reverse=False,
                                final=(4 * a_blk, o_b, a_norm_g[l]))
        y_f = _fourier_mixer(z, 5 * a_blk)
        cu = (5 * d_a + d_f) // d_c
        y_c = _chunk_mlp_mixer(z, cu, cu + 1, sg_norm_g[l], sg_norm_b[l], w_spatial[l], b_spatial[l])
        return (o_a, y_f, y_c), sf_f, sf_b

    def rest_of_layer(xs, ys, m, l, width):
        sh2, sc2 = m[3], m[4]
        x1, h2 = _out_projection(*ys, w_out_b[l], xs, m[2], ln1_g[l], ln1_b[l], sh2, sc2, alpha)
        hid = _ffn_up(h2, w_gate_b[l], w_up_b[l], conv_w[l], conv_b[l], width)
        return _ffn_down(hid, w_down_b[l], x1, m[5], ln2_g[l], ln2_b[l], alpha)

    for l in range(depth):
        last = l == depth - 1
        m_all = [mods[l, :, i * d:(i + 1) * d] for i in range(6)]
        m_x = [t[:batch].reshape(batch, 1, d) for t in m_all]
        m_c = [t[batch:batch + 1].reshape(1, 1, d) for t in m_all]

        zc = _in_projection(ctx, m_c[0], m_c[1], w_in_b[l])
        ys_c, s_f, s_b = mixers(zc, zero_state, zero_state, l, with_output=not last)
        zx = _in_projection(x, m_x[0], m_x[1], w_in_b[l])
        ys_x, _, _ = mixers(zx, s_f, s_b, l)
        x = rest_of_layer(x, ys_x, m_x, l, GRID_W)
        if not last:
            ctx = rest_of_layer(ctx, ys_c, m_c, l, ctx.shape[1])
    return x
```

```python
import functools
import math

import numpy as np
import jax
import jax.numpy as jnp
from jax import lax
from jax.experimental import pallas as pl
from jax.experimental.pallas import tpu as pltpu

F32 = jnp.float32
BF16 = jnp.bfloat16

LANES = 128
SUBLANES = 8
VMEM_LIMIT_BYTES = 56 * 1024 * 1024

A_HEADS = 8
F_GROUPS = 4
C_GROUPS = 4
MLP_CHUNK = 128
SCAN_BLOCK = 128
GRID_W = 64
LN_EPS = 1e-6
RMS_EPS = 1e-6


def _cparams(*sem):
    return pltpu.CompilerParams(dimension_semantics=sem, vmem_limit_bytes=VMEM_LIMIT_BYTES)


def _silu(x):
    return x * jax.nn.sigmoid(x)


def _layer_norm(x):
    mu = jnp.mean(x, axis=-1, keepdims=True)
    xc = x - mu
    var = jnp.mean(xc * xc, axis=-1, keepdims=True)
    return xc * lax.rsqrt(var + LN_EPS)


def _cond_spec(arr):
    d = arr.shape[-1]
    if arr.shape[0] == 1:
        return pl.BlockSpec((1, 1, d), lambda b, *_: (0, 0, 0))
    return pl.BlockSpec((1, 1, d), lambda b, *_: (b, 0, 0))


def _ada_kernel(cond_ref, w_ref, b_ref, o_ref):
    s = _silu(cond_ref[...]).astype(BF16)
    o_ref[0] = jnp.dot(s, w_ref[0].astype(BF16), preferred_element_type=F32) + b_ref[0]


def _ada_modulation(cond, w_ada, b_ada):
    depth, d, n = w_ada.shape
    rows = cond.shape[0]
    tn = 1024
    return pl.pallas_call(
        _ada_kernel,
        grid=(depth, n // tn),
        in_specs=[
            pl.BlockSpec((rows, d), lambda l, j: (0, 0)),
            pl.BlockSpec((1, d, tn), lambda l, j: (l, 0, j)),
            pl.BlockSpec((1, 1, tn), lambda l, j: (l, 0, j)),
        ],
        out_specs=pl.BlockSpec((1, rows, tn), lambda l, j: (l, 0, j)),
        out_shape=jax.ShapeDtypeStruct((depth, rows, n), F32),
        compiler_params=_cparams("parallel", "parallel"),
        name="ada_modulation",
    )(cond, w_ada, b_ada.reshape(depth, 1, n))


def _inproj_kernel(x_ref, sh_ref, sc_ref, w_ref, o_ref, h_scr):
    @pl.when(pl.program_id(2) == 0)
    def _():
        h = _layer_norm(x_ref[0]) * (1.0 + sc_ref[0]) + sh_ref[0]
        h_scr[...] = h.astype(BF16)

    o_ref[0] = jnp.dot(h_scr[...], w_ref[...], preferred_element_type=F32)


def _in_projection(x, shift, scale, w):
    b, l, d = x.shape
    n = w.shape[1]
    tm = min(l, 1024)
    tn = 512
    return pl.pallas_call(
        _inproj_kernel,
        grid=(b, l // tm, n // tn),
        in_specs=[
            pl.BlockSpec((1, tm, d), lambda b, i, j: (b, i, 0)),
            _cond_spec(shift),
            _cond_spec(scale),
            pl.BlockSpec((d, tn), lambda b, i, j: (0, j)),
        ],
        out_specs=pl.BlockSpec((1, tm, tn), lambda b, i, j: (b, i, j)),
        out_shape=jax.ShapeDtypeStruct((b, l, n), F32),
        scratch_shapes=[pltpu.VMEM((tm, d), BF16)],
        compiler_params=_cparams("parallel", "parallel", "arbitrary"),
        name="in_projection",
    )(x, shift, scale, w)


LOG2E = 1.4426950408889634
SCAN_HEADS_PER_STEP = 2
_HIGH_LEVELS = (8, 16, 32, 64)
_M_DIAG, _M_G1, _M_G2, _M_G4, _M_Q1, _M_K1, _M_Q2, _M_Q4, _M_H8, _M_H16, _M_H32 = range(11)


def _scan_constants(reverse):
    nb = SCAN_BLOCK
    row = np.arange(nb)[:, None]
    col = np.arange(nb)[None, :]
    x = row ^ col
    tabs = np.zeros((11, nb, nb), np.float32)
    tabs[_M_DIAG] = x == 0
    tabs[_M_G1], tabs[_M_G2], tabs[_M_G4] = x < 2, x < 4, x < 8

    def q_role(m):
        upper = (row & m) == 0
        return np.broadcast_to(upper if reverse else ~upper, (nb, nb))

    tabs[_M_Q1], tabs[_M_Q2], tabs[_M_Q4] = q_role(1), q_role(2), q_role(4)
    tabs[_M_K1] = 1.0 - tabs[_M_Q1]
    for idx, m in ((_M_H8, 8), (_M_H16, 16), (_M_H32, 32)):
        grp = np.arange(nb // 2)[:, None] // m
        tabs[idx, :nb // 2] = (col // (2 * m)) == grp
    tri = (col >= row) if reverse else (col <= row)
    return jnp.asarray(tabs), jnp.asarray(tri, BF16)


def _dot_nt(a, b):
    return lax.dot_general(a, b, (((1,), (1,)), ((), ())), preferred_element_type=F32)


def _dot_tn(a, b):
    return lax.dot_general(a, b, (((0,), (0,)), ((), ())), preferred_element_type=F32)


def _scan_block(zq, v, zf, lb, s_t, c_scr, tab_ref, tri, *, reverse):
    nb = SCAN_BLOCK
    half = nb // 2
    q = _silu(zq)

    log_lb = jnp.log(lb)
    e = jnp.exp(-jnp.abs(zf))
    a = jnp.minimum(zf, 0.0) - jnp.log(1.0 + e) + jnp.log1p(-lb)
    logf = jnp.maximum(a, log_lb) + jnp.log(1.0 + jnp.exp(-jnp.abs(a - log_lb)))
    k = (1.0 - lb) * (jnp.where(zf > 0.0, e, 1.0) / (1.0 + e))
    f = 1.0 - k

    g2 = logf * LOG2E
    g_hi = g2.astype(BF16)
    r1 = g2 - g_hi.astype(F32)
    g_mid = r1.astype(BF16)
    g_lo = (r1 - g_mid.astype(F32)).astype(BF16)
    c2 = (jnp.dot(tri, g_hi, preferred_element_type=F32)
          + jnp.dot(tri, g_mid, preferred_element_type=F32)
          + jnp.dot(tri, g_lo, preferred_element_type=F32))
    c_scr[...] = c2

    def crow(r):
        return c_scr[r:r + 1, :]

    tile = lambda x, i: x[i * SUBLANES:(i + 1) * SUBLANES]
    n_tiles = nb // SUBLANES

    lvl = _dot_nt(q.astype(BF16), k.astype(BF16)) * tab_ref[_M_DIAG]
    qt = (q * f * tab_ref[_M_Q1]).astype(BF16)
    kt = (k * tab_ref[_M_K1]).astype(BF16)
    lvl = lvl + _dot_nt(qt, kt) * tab_ref[_M_G1]

    upper_half_tile = lax.broadcasted_iota(jnp.int32, (SUBLANES, LANES), 0) >= SUBLANES // 2
    for m, mq, mg in ((2, _M_Q2, _M_G2), (4, _M_Q4, _M_G4)):
        refs = []
        for t in range(n_tiles):
            r0 = t * SUBLANES + (m if reverse else m - 1)
            rb = jnp.broadcast_to(crow(r0), (SUBLANES, LANES))
            if m == 2:
                rb = jnp.where(upper_half_tile, jnp.broadcast_to(crow(r0 + 4), (SUBLANES, LANES)), rb)
            refs.append(rb)
        w = jnp.exp2(-jnp.abs(c2 - jnp.concatenate(refs, axis=0)))
        wq = w * tab_ref[mq]
        wk = w - wq
        lvl = lvl + _dot_nt((q * wq).astype(BF16), (k * wk).astype(BF16)) * tab_ref[mg]
    score_tiles = [tile(lvl, i) for i in range(n_tiles)]

    for m, mh in zip(_HIGH_LEVELS, (_M_H8, _M_H16, _M_H32, None)):
        q_parts, k_parts, q_tiles = [], [], []
        for g in range(nb // (2 * m)):
            base = g * 2 * m
            first, second = slice(base, base + m), slice(base + m, base + 2 * m)
            q_rows, k_rows = (first, second) if reverse else (second, first)
            c_ref = crow(base + m if reverse else base + m - 1)
            q_parts.append(q[q_rows] * jnp.exp2(c2[q_rows] - c_ref))
            k_part = k[k_rows] * jnp.exp2(c_ref - c2[k_rows])
            zeros = jnp.zeros((m, LANES), F32)
            k_parts += [zeros, k_part] if reverse else [k_part, zeros]
            q_tiles += list(range(q_rows.start // SUBLANES, q_rows.stop // SUBLANES))
        res = _dot_nt(jnp.concatenate(q_parts, axis=0).astype(BF16),
                      jnp.concatenate(k_parts, axis=0).astype(BF16))
        if mh is not None:
            res = res * tab_ref[mh, 0:half, :]
        for j, t in enumerate(q_tiles):
            score_tiles[t] = score_tiles[t] + tile(res, j)
    scores = jnp.concatenate(score_tiles, axis=0)

    q_in = (q * jnp.exp2(c2)).astype(BF16)
    o = _dot_nt(q_in, s_t.astype(BF16)) + jnp.dot(scores.astype(BF16), v.astype(BF16),
                                                     preferred_element_type=F32)
    c_tot = crow(0 if reverse else nb - 1)
    k_dec = (k * jnp.exp2(c_tot - c2)).astype(BF16)
    s_new = s_t * jnp.exp2(c_tot) + _dot_tn(v.astype(BF16), k_dec)
    return o, s_new


def _scan_kernel(*refs, reverse, finalize, n_sub):
    if finalize:
        (zq_ref, zv_ref, zf_ref, lb_ref, s0_ref, tab_ref, tri_ref, zg_ref, oo_ref, ng_ref,
         o_ref, sfin_ref, s_scr, c_scr) = refs
    else:
        zq_ref, zv_ref, zf_ref, lb_ref, s0_ref, tab_ref, tri_ref, o_ref, sfin_ref, s_scr, c_scr = refs
    nb = SCAN_BLOCK
    ci = pl.program_id(2)

    @pl.when(ci == 0)
    def _():
        s_scr[...] = s0_ref[0]

    tri = tri_ref[...]
    hps = SCAN_HEADS_PER_STEP
    states = [s_scr[hh] for hh in range(hps)]
    for i in range(n_sub):
        j = (n_sub - 1 - i) if reverse else i
        rows = slice(j * nb, (j + 1) * nb)
        for hh in range(hps):
            cols = slice(hh * LANES, (hh + 1) * LANES)
            o, states[hh] = _scan_block(zq_ref[0, rows, cols], zv_ref[0, rows, cols], zf_ref[0, rows, cols],
                                        lb_ref[0, :, cols], states[hh], c_scr.at[j * hps + hh], tab_ref, tri,
                                        reverse=reverse)
            if finalize:
                o = o + oo_ref[0, rows, cols]
                ms = jnp.mean(o * o, axis=-1, keepdims=True)
                o = o * lax.rsqrt(ms + RMS_EPS) * ng_ref[0, :, cols] * _silu(zg_ref[0, rows, cols])
            o_ref[0, rows, cols] = o.astype(o_ref.dtype)
    for hh in range(hps):
        s_scr[hh] = states[hh]

    @pl.when(ci == pl.num_programs(2) - 1)
    def _():
        sfin_ref[0] = s_scr[...]


def _hgrn2_scan(z, col_q, col_v, col_f, lb, s0, *, reverse, final=None):
    b, l, _ = z.shape
    hps = SCAN_HEADS_PER_STEP
    heads, hd, wd = A_HEADS, LANES, SCAN_HEADS_PER_STEP * LANES
    cb = min(l, 1024)
    nc = l // cb
    n_sub = cb // SCAN_BLOCK
    assert heads % hps == 0 and col_q % hps == 0 and col_v % hps == 0 and col_f % hps == 0

    def chunk(i):
        return (nc - 1 - i) if reverse else i

    def zspec(col0):
        return pl.BlockSpec((1, cb, wd), lambda b, h, i: (b, chunk(i), col0 // hps + h))

    def whole(a):
        return pl.BlockSpec(a.shape, lambda b, h, i: (0,) * a.ndim)

    head_row = pl.BlockSpec((1, 1, wd), lambda b, h, i: (h, 0, 0))
    state = pl.BlockSpec((1, hps, hd, hd), lambda b, h, i: (b, h, 0, 0))
    tabs, tri = _scan_constants(reverse)
    in_specs = [zspec(col_q), zspec(col_v), zspec(col_f), head_row, state, whole(tabs), whole(tri)]
    args = [z, z, z, lb.reshape(heads // hps, 1, wd), s0, tabs, tri]
    if final is not None:
        col_g, other, norm_g = final
        assert col_g % hps == 0
        in_specs += [zspec(col_g), zspec(0), head_row]
        args += [z, other, norm_g.reshape(heads // hps, 1, wd)]
    out_dtype = BF16 if final is not None else F32
    return pl.pallas_call(
        functools.partial(_scan_kernel, reverse=reverse, finalize=final is not None, n_sub=n_sub),
        grid=(b, heads // hps, nc),
        in_specs=in_specs,
        out_specs=[zspec(0), state],
        out_shape=[jax.ShapeDtypeStruct((b, l, heads * hd), out_dtype),
                   jax.ShapeDtypeStruct((b, heads, hd, hd), F32)],
        scratch_shapes=[pltpu.VMEM((hps, hd, hd), F32), pltpu.VMEM((n_sub * hps, SCAN_BLOCK, hd), F32)],
        compiler_params=_cparams("parallel", "parallel", "arbitrary"),
        name="hgrn2_scan_bwd" if reverse else "hgrn2_scan_fwd",
    )(*args)


def _dft_cos_sin(n):
    idx = np.arange(n, dtype=np.int64)
    ang = 2.0 * np.pi * ((idx[:, None] * idx[None, :]) % n).astype(np.float64) / n
    return np.cos(ang), np.sin(ang)


def _fourier_two_stage_kernel(z_ref, csc_ref, csq_ref, csr_ref, tw1_ref, o_ref, tre_scr, tim_scr, *, q):
    nr = LANES
    csc = csc_ref[...]
    csq = csq_ref[...]
    tw_c1 = tw1_ref[0]
    tw_s1 = tw1_ref[1]

    def stage_a(r, tw):
        tw_c, tw_s = tw
        x = z_ref[0, pl.ds(r, q, stride=nr), :]
        uv = jnp.dot(x.astype(BF16), csc, preferred_element_type=F32)
        res = jnp.dot(csq, uv.astype(BF16), preferred_element_type=F32)
        cu, cv = res[:q, :LANES], res[:q, LANES:]
        su, sv = res[q:, :LANES], res[q:, LANES:]
        g_re = cu - sv
        g_im = -(cv + su)
        tre_scr[pl.ds(r, q, stride=nr), :] = g_re * tw_c + g_im * tw_s
        tim_scr[pl.ds(r, q, stride=nr), :] = g_im * tw_c - g_re * tw_s
        return tw_c * tw_c1 - tw_s * tw_s1, tw_s * tw_c1 + tw_c * tw_s1

    lax.fori_loop(0, nr, stage_a, (jnp.ones((q, LANES), F32), jnp.zeros((q, LANES), F32)), unroll=4)

    csr = csr_ref[...]

    def stage_b(ka, carry):
        off = pl.multiple_of(ka * nr, nr)
        t = jnp.concatenate([tre_scr[pl.ds(off, nr), :], tim_scr[pl.ds(off, nr), :]], axis=0)
        y = jnp.dot(csr, t.astype(BF16), preferred_element_type=F32)
        o_ref[0, pl.ds(ka, nr, stride=q), :] = y
        return carry

    lax.fori_loop(0, q, stage_b, 0, unroll=4)


def _fourier_direct_kernel(z_ref, csc_ref, csl_ref, o_ref):
    uv = jnp.dot(z_ref[0].astype(BF16), csc_ref[...], preferred_element_type=F32)
    t = jnp.concatenate([uv[:, :LANES], uv[:, LANES:]], axis=0)
    o_ref[0] = jnp.dot(csl_ref[...], t.astype(BF16), preferred_element_type=F32)


def _fourier_mixer(z, col0):
    b, l, _ = z.shape
    scale = 1.0 / math.sqrt(l * LANES)
    cc, sc = _dft_cos_sin(LANES)
    csc = jnp.asarray(np.concatenate([cc, sc], axis=1) * scale, BF16)
    out_shape = jax.ShapeDtypeStruct((b, l, F_GROUPS * LANES), F32)
    slab = pl.BlockSpec((1, l, LANES), lambda b, g: (b, 0, col0 + g))
    out_slab = pl.BlockSpec((1, l, LANES), lambda b, g: (b, 0, g))

    def whole(a):
        return pl.BlockSpec(a.shape, lambda b, g: (0,) * a.ndim)

    if l <= 1024:
        cl, sl = _dft_cos_sin(l)
        csl = jnp.asarray(np.concatenate([cl, -sl], axis=1), BF16)
        return pl.pallas_call(
            _fourier_direct_kernel,
            grid=(b, F_GROUPS),
            in_specs=[slab, whole(csc), whole(csl)],
            out_specs=out_slab,
            out_shape=out_shape,
            compiler_params=_cparams("parallel", "parallel"),
            name="fourier_direct",
        )(z, csc, csl)

    q = l // LANES
    cq, sq = _dft_cos_sin(q)
    csq = jnp.asarray(np.concatenate([cq, sq], axis=0), BF16)
    csr = jnp.asarray(np.concatenate([cc, sc], axis=1), BF16)
    ang1 = 2.0 * np.pi * np.arange(q, dtype=np.float64) / l
    tw1 = jnp.asarray(np.stack([np.broadcast_to(np.cos(ang1)[:, None], (q, LANES)),
                                np.broadcast_to(np.sin(ang1)[:, None], (q, LANES))]), F32)
    return pl.pallas_call(
        functools.partial(_fourier_two_stage_kernel, q=q),
        grid=(b, F_GROUPS),
        in_specs=[pl.BlockSpec((1, l, LANES), lambda b, g: (b, 0, col0 + g), pipeline_mode=pl.Buffered(1)),
                  whole(csc), whole(csq), whole(csr), whole(tw1)],
        out_specs=pl.BlockSpec((1, l, LANES), lambda b, g: (b, 0, g), pipeline_mode=pl.Buffered(1)),
        out_shape=out_shape,
        scratch_shapes=[pltpu.VMEM((l, LANES), F32), pltpu.VMEM((l, LANES), F32)],
        compiler_params=_cparams("parallel", "parallel"),
        name="fourier_two_stage",
    )(z, csc, csq, csr, tw1)


def _gelu(x):
    return jax.nn.gelu(x, approximate=True)


def _chunk_mlp_kernel(zu_ref, zv_ref, g_ref, b_ref, ws_ref, bs_ref, o_ref, *, n_chunks):
    v = _layer_norm(_gelu(zv_ref[0])) * g_ref[...] + b_ref[...]
    vb = v.astype(BF16)
    for c in range(n_chunks):
        rows = slice(c * MLP_CHUNK, (c + 1) * MLP_CHUNK)
        for g in range(C_GROUPS):
            cols = slice(g * LANES, (g + 1) * LANES)
            sv = jnp.dot(ws_ref[g], vb[rows, cols], preferred_element_type=F32) + bs_ref[g]
            o_ref[0, rows, cols] = (_gelu(zu_ref[0, rows, cols]) * sv).astype(o_ref.dtype)


def _chunk_mlp_mixer(z, col_u, col_v, ln_g, ln_b, w_s, b_s):
    b, l, _ = z.shape
    dc = C_GROUPS * LANES
    tm = min(l, 512)
    bs_full = jnp.broadcast_to(b_s[:, :, None], (C_GROUPS, MLP_CHUNK, LANES)).astype(F32)

    def whole(a):
        return pl.BlockSpec(a.shape, lambda b, i: (0,) * a.ndim)

    ws = w_s.astype(BF16)
    g2, b2 = ln_g.reshape(1, dc), ln_b.reshape(1, dc)
    return pl.pallas_call(
        functools.partial(_chunk_mlp_kernel, n_chunks=tm // MLP_CHUNK),
        grid=(b, l // tm),
        in_specs=[pl.BlockSpec((1, tm, dc), lambda b, i: (b, i, col_u)),
                  pl.BlockSpec((1, tm, dc), lambda b, i: (b, i, col_v)),
                  whole(g2), whole(b2), whole(ws), whole(bs_full)],
        out_specs=pl.BlockSpec((1, tm, dc), lambda b, i: (b, i, 0)),
        out_shape=jax.ShapeDtypeStruct((b, l, dc), BF16),
        compiler_params=_cparams("parallel", "parallel"),
        name="chunk_mlp_mixer",
    )(z, z, g2, b2, ws, bs_full)


def _outproj_kernel(oa_ref, yf_ref, yc_ref, wa_ref, wf_ref, wc_ref, x_ref, gate_ref, lng_ref, lnb_ref,
                    sh_ref, sc_ref, x1_ref, h_ref, *, alpha):
    y = jnp.dot(oa_ref[0], wa_ref[...], preferred_element_type=F32)
    y = y + jnp.dot(yf_ref[0].astype(BF16), wf_ref[...], preferred_element_type=F32)
    y = y + jnp.dot(yc_ref[0], wc_ref[...], preferred_element_type=F32)
    x1 = _layer_norm(alpha * x_ref[0] + gate_ref[0] * y) * lng_ref[...] + lnb_ref[...]
    x1_ref[0] = x1
    h_ref[0] = (_layer_norm(x1) * (1.0 + sc_ref[0]) + sh_ref[0]).astype(BF16)


def _out_projection(o_a, y_f, y_c, w_out, x, gate, ln_g, ln_b, shift2, scale2, alpha):
    b, l, d = x.shape
    da, df, dc = o_a.shape[-1], y_f.shape[-1], y_c.shape[-1]
    tm = min(l, 512)
    row = lambda w: pl.BlockSpec((1, tm, w), lambda b, i: (b, i, 0))

    def wspec(rows, blk):
        return pl.BlockSpec((rows, d), lambda b, i: (blk, 0), pipeline_mode=pl.Buffered(1))

    vec = pl.BlockSpec((1, d), lambda b, i: (0, 0))
    assert da % df == 0 and df == dc
    return pl.pallas_call(
        functools.partial(_outproj_kernel, alpha=alpha),
        grid=(b, l // tm),
        in_specs=[row(da), row(df), row(dc),
                  wspec(da, 0), wspec(df, da // df), wspec(dc, da // df + 1),
                  row(d), _cond_spec(gate), vec, vec, _cond_spec(shift2), _cond_spec(scale2)],
        out_specs=[row(d), row(d)],
        out_shape=[jax.ShapeDtypeStruct((b, l, d), F32), jax.ShapeDtypeStruct((b, l, d), BF16)],
        compiler_params=_cparams("parallel", "parallel"),
        name="out_projection",
    )(o_a, y_f, y_c, w_out, w_out, w_out, x, gate, ln_g.reshape(1, d), ln_b.reshape(1, d), shift2, scale2)


def _ffn_up_kernel(*refs, width, vertical, tm):
    if vertical:
        h_ref, hp_ref, hn_ref, wg_ref, wu_ref, cw_ref, cb_ref, o_ref, g_scr = refs
    else:
        h_ref, wg_ref, wu_ref, cw_ref, cb_ref, o_ref, g_scr = refs
    i = pl.program_id(1)
    wg = wg_ref[...]
    h = h_ref[0]
    halo = width if vertical else 0
    g_scr[halo:halo + tm, :] = jnp.dot(h, wg, preferred_element_type=F32)
    if vertical:
        not_first = (i > 0).astype(F32)
        not_last = (i < pl.num_programs(1) - 1).astype(F32)
        g_scr[0:halo, :] = jnp.dot(hp_ref[0], wg, preferred_element_type=F32) * not_first
        g_scr[halo + tm:, :] = jnp.dot(hn_ref[0], wg, preferred_element_type=F32) * not_last
    ext = g_scr[...]
    n_ext = ext.shape[0]
    colpos = lax.broadcasted_iota(jnp.int32, ext.shape, 0) % width
    left = jnp.where(colpos == 0, 0.0, pltpu.roll(ext, 1, axis=0))
    right = jnp.where(colpos == width - 1, 0.0, pltpu.roll(ext, n_ext - 1, axis=0))
    taps = (left, ext, right)
    acc = None
    for dh in (range(3) if vertical else (1,)):
        base = halo + (dh - 1) * width
        for dw in range(3):
            term = taps[dw][base:base + tm, :] * cw_ref[dh * 3 + dw:dh * 3 + dw + 1, :]
            acc = term if acc is None else acc + term
    gate = _silu(acc + cb_ref[...])
    up = jnp.dot(h, wu_ref[...], preferred_element_type=F32)
    o_ref[0] = (gate * up).astype(o_ref.dtype)


def _ffn_up(h, w_gate, w_up, conv_w, conv_b, width):
    b, l, d = h.shape
    f = w_gate.shape[1]
    vertical = l > width
    tm = min(l, 1024)
    tf = 512
    assert tm % width == 0 and f % tf == 0
    rows_per_tile = tm // width
    n_rows = l // width
    in_specs = [pl.BlockSpec((1, tm, d), lambda b, i, j: (b, i, 0))]
    args = [h]
    if vertical:
        in_specs += [
            pl.BlockSpec((1, width, d), lambda b, i, j: (b, jnp.maximum(i * rows_per_tile - 1, 0), 0)),
            pl.BlockSpec((1, width, d), lambda b, i, j: (b, jnp.minimum((i + 1) * rows_per_tile, n_rows - 1), 0)),
        ]
        args += [h, h]
    wspec = pl.BlockSpec((d, tf), lambda b, i, j: (0, j))
    in_specs += [wspec, wspec,
                 pl.BlockSpec((9, tf), lambda b, i, j: (0, j)),
                 pl.BlockSpec((1, tf), lambda b, i, j: (0, j))]
    args += [w_gate, w_up, conv_w.reshape(9, f), conv_b.reshape(1, f)]
    ext_rows = tm + (2 * width if vertical else 0)
    return pl.pallas_call(
        functools.partial(_ffn_up_kernel, width=width, vertical=vertical, tm=tm),
        grid=(b, l // tm, f // tf),
        in_specs=in_specs,
        out_specs=pl.BlockSpec((1, tm, tf), lambda b, i, j: (b, i, j)),
        out_shape=jax.ShapeDtypeStruct((b, l, f), BF16),
        scratch_shapes=[pltpu.VMEM((ext_rows, tf), F32)],
        compiler_params=_cparams("parallel", "parallel", "arbitrary"),
        name="ffn_up",
    )(*args)


def _ffn_down_kernel(hid_ref, w_ref, x_ref, gate_ref, lng_ref, lnb_ref, o_ref, *, alpha):
    f = jnp.dot(hid_ref[0], w_ref[...], preferred_element_type=F32)
    y = alpha * x_ref[0] + gate_ref[0] * f
    o_ref[0] = _layer_norm(y) * lng_ref[...] + lnb_ref[...]


def _ffn_down(hid, w_down, x, gate, ln_g, ln_b, alpha):
    b, l, f = hid.shape
    d = x.shape[-1]
    tm = min(l, 256)
    vec = pl.BlockSpec((1, d), lambda b, i: (0, 0))
    return pl.pallas_call(
        functools.partial(_ffn_down_kernel, alpha=alpha),
        grid=(b, l // tm),
        in_specs=[pl.BlockSpec((1, tm, f), lambda b, i: (b, i, 0)),
                  pl.BlockSpec((f, d), lambda b, i: (0, 0), pipeline_mode=pl.Buffered(1)),
                  pl.BlockSpec((1, tm, d), lambda b, i: (b, i, 0)),
                  _cond_spec(gate), vec, vec],
        out_specs=pl.BlockSpec((1, tm, d), lambda b, i: (b, i, 0)),
        out_shape=jax.ShapeDtypeStruct((b, l, d), F32),
        compiler_params=_cparams("parallel", "parallel"),
        name="ffn_down",
    )(hid, w_down, x, gate, ln_g.reshape(1, d), ln_b.reshape(1, d))


def kernel(x, c, ctx, c_ctx, w_ada, b_ada, w_in, lower_bounds, a_norm_g, sg_norm_g, sg_norm_b,
           w_spatial, b_spatial, w_out, ln1_g, ln1_b, w_gate, w_up, conv_w, conv_b, w_down, ln2_g, ln2_b):
    batch, seq, d = x.shape
    depth = w_in.shape[0]
    d_a = a_norm_g.shape[-1]
    d_c = sg_norm_g.shape[-1]
    d_f = w_in.shape[-1] - 5 * d_a - 2 * d_c
    assert d_a == A_HEADS * LANES and d_c == C_GROUPS * LANES and d_f == F_GROUPS * LANES
    alpha = (2 * depth) ** 0.25
    a_blk = d_a // LANES

    lbs = jnp.cumsum(jax.nn.softmax(lower_bounds.astype(F32), axis=1), axis=1)
    lbs = lbs - lbs[:, :1]

    rows = -(-(batch + 1) // SUBLANES) * SUBLANES
    cond = jnp.zeros((rows, d), F32).at[:batch].set(c).at[batch].set(c_ctx)
    mods = _ada_modulation(cond, w_ada, b_ada)

    w_in_b, w_out_b = w_in.astype(BF16), w_out.astype(BF16)
    w_gate_b, w_up_b, w_down_b = w_gate.astype(BF16), w_up.astype(BF16), w_down.astype(BF16)
    zero_state = jnp.zeros((batch, A_HEADS, LANES, LANES), F32)

    def mixers(z, s_fwd, s_bwd, l, with_output=True):
        o_b, sf_b = _hgrn2_scan(z, 0, a_blk, 3 * a_blk, lbs[1, l], s_bwd, reverse=True)
        if not with_output:
            _, sf_f = _hgrn2_scan(z, 0, a_blk, 2 * a_blk, lbs[0, l], s_fwd, reverse=False)
            return None, sf_f, sf_b
        o_a, sf_f = _hgrn2_scan(z, 0, a_blk, 2 * a_blk, lbs[0, l], s_fwd, reverse=False,
                                final=(4 * a_blk, o_b, a_norm_g[l]))
        y_f = _fourier_mixer(z, 5 * a_blk)
        cu = (5 * d_a + d_f) // d_c
        y_c = _chunk_mlp_mixer(z, cu, cu + 1, sg_norm_g[l], sg_norm_b[l], w_spatial[l], b_spatial[l])
        return (o_a, y_f, y_c), sf_f, sf_b

    def rest_of_layer(xs, ys, m, l, width):
        sh2, sc2 = m[3], m[4]
        x1, h2 = _out_projection(*ys, w_out_b[l], xs, m[2], ln1_g[l], ln1_b[l], sh2, sc2, alpha)
        hid = _ffn_up(h2, w_gate_b[l], w_up_b[l], conv_w[l], conv_b[l], width)
        return _ffn_down(hid, w_down_b[l], x1, m[5], ln2_g[l], ln2_b[l], alpha)

    for l in range(depth):
        last = l == depth - 1
        m_all = [mods[l, :, i * d:(i + 1) * d] for i in range(6)]
        m_x = [t[:batch].reshape(batch, 1, d) for t in m_all]
        m_c = [t[batch:batch + 1].reshape(1, 1, d) for t in m_all]

        zc = _in_projection(ctx, m_c[0], m_c[1], w_in_b[l])
        ys_c, s_f, s_b = mixers(zc, zero_state, zero_state, l, with_output=not last)
        zx = _in_projection(x, m_x[0], m_x[1], w_in_b[l])
        ys_x, _, _ = mixers(zx, s_f, s_b, l)
        x = rest_of_layer(x, ys_x, m_x, l, GRID_W)
        if not last:
            ctx = rest_of_layer(ctx, ys_c, m_c, l, ctx.shape[1])
    return x
```

```python
import functools
import math

import numpy as np
import jax
import jax.numpy as jnp
from jax import lax
from jax.experimental import pallas as pl
from jax.experimental.pallas import tpu as pltpu

F32 = jnp.float32
BF16 = jnp.bfloat16

LANES = 128
SUBLANES = 8
VMEM_LIMIT_BYTES = 56 * 1024 * 1024

A_HEADS = 8
F_GROUPS = 4
C_GROUPS = 4
MLP_CHUNK = 128
SCAN_BLOCK = 128
GRID_W = 64
LN_EPS = 1e-6
RMS_EPS = 1e-6


def _cparams(*sem):
    return pltpu.CompilerParams(dimension_semantics=sem, vmem_limit_bytes=VMEM_LIMIT_BYTES)


def _silu(x):
    return x * jax.nn.sigmoid(x)


def _layer_norm(x):
    mu = jnp.mean(x, axis=-1, keepdims=True)
    xc = x - mu
    var = jnp.mean(xc * xc, axis=-1, keepdims=True)
    return xc * lax.rsqrt(var + LN_EPS)


def _cond_spec(arr):
    d = arr.shape[-1]
    if arr.shape[0] == 1:
        return pl.BlockSpec((1, 1, d), lambda b, *_: (0, 0, 0))
    return pl.BlockSpec((1, 1, d), lambda b, *_: (b, 0, 0))


def _ada_kernel(cond_ref, w_ref, b_ref, o_ref):
    s = _silu(cond_ref[...]).astype(BF16)
    o_ref[0] = jnp.dot(s, w_ref[0].astype(BF16), preferred_element_type=F32) + b_ref[0]


def _ada_modulation(cond, w_ada, b_ada):
    depth, d, n = w_ada.shape
    rows = cond.shape[0]
    tn = 1024
    return pl.pallas_call(
        _ada_kernel,
        grid=(depth, n // tn),
        in_specs=[
            pl.BlockSpec((rows, d), lambda l, j: (0, 0)),
            pl.BlockSpec((1, d, tn), lambda l, j: (l, 0, j)),
            pl.BlockSpec((1, 1, tn), lambda l, j: (l, 0, j)),
        ],
        out_specs=pl.BlockSpec((1, rows, tn), lambda l, j: (l, 0, j)),
        out_shape=jax.ShapeDtypeStruct((depth, rows, n), F32),
        compiler_params=_cparams("parallel", "parallel"),
        name="ada_modulation",
    )(cond, w_ada, b_ada.reshape(depth, 1, n))


def _inproj_kernel(x_ref, sh_ref, sc_ref, w_ref, o_ref, h_scr):
    @pl.when(pl.program_id(2) == 0)
    def _():
        h = _layer_norm(x_ref[0]) * (1.0 + sc_ref[0]) + sh_ref[0]
        h_scr[...] = h.astype(BF16)

    o_ref[0] = jnp.dot(h_scr[...], w_ref[...], preferred_element_type=F32)


def _in_projection(x, shift, scale, w):
    b, l, d = x.shape
    n = w.shape[1]
    tm = min(l, 1024)
    tn = 512
    return pl.pallas_call(
        _inproj_kernel,
        grid=(b, l // tm, n // tn),
        in_specs=[
            pl.BlockSpec((1, tm, d), lambda b, i, j: (b, i, 0)),
            _cond_spec(shift),
            _cond_spec(scale),
            pl.BlockSpec((d, tn), lambda b, i, j: (0, j)),
        ],
        out_specs=pl.BlockSpec((1, tm, tn), lambda b, i, j: (b, i, j)),
        out_shape=jax.ShapeDtypeStruct((b, l, n), F32),
        scratch_shapes=[pltpu.VMEM((tm, d), BF16)],
        compiler_params=_cparams("parallel", "parallel", "arbitrary"),
        name="in_projection",
    )(x, shift, scale, w)


LOG2E = 1.4426950408889634
SCAN_HEADS_PER_STEP = 2
SCAN_GROUP = 4
_HIGH_LEVELS = (8, 16, 32, 64)
_M_DIAG, _M_G1, _M_G2, _M_G4, _M_Q1, _M_K1, _M_Q2, _M_Q4, _M_H8, _M_H16, _M_H32 = range(11)


def _scan_constants(reverse):
    nb = SCAN_BLOCK
    row = np.arange(nb)[:, None]
    col = np.arange(nb)[None, :]
    x = row ^ col
    tabs = np.zeros((11, nb, nb), np.float32)
    tabs[_M_DIAG] = x == 0
    tabs[_M_G1], tabs[_M_G2], tabs[_M_G4] = x < 2, x < 4, x < 8

    def q_role(m):
        upper = (row & m) == 0
        return np.broadcast_to(upper if reverse else ~upper, (nb, nb))

    tabs[_M_Q1], tabs[_M_Q2], tabs[_M_Q4] = q_role(1), q_role(2), q_role(4)
    tabs[_M_K1] = 1.0 - tabs[_M_Q1]
    for idx, m in ((_M_H8, 8), (_M_H16, 16), (_M_H32, 32)):
        grp = np.arange(nb // 2)[:, None] // m
        tabs[idx, :nb // 2] = (col // (2 * m)) == grp
    tri = (col >= row) if reverse else (col <= row)
    return jnp.asarray(tabs), jnp.asarray(np.concatenate([tri, tri, tri], axis=1), BF16)


def _dot_nt(a, b):
    return lax.dot_general(a, b, (((1,), (1,)), ((), ())), preferred_element_type=F32)


def _dot_tn(a, b):
    return lax.dot_general(a, b, (((0,), (0,)), ((), ())), preferred_element_type=F32)


def _scan_gates(zq, zf, lb, c_slot, tri3):
    q = _silu(zq)
    l2_lb = jnp.log2(lb)
    e = jnp.exp2(-LOG2E * jnp.abs(zf))
    a2 = jnp.minimum(zf, 0.0) * LOG2E - jnp.log2(1.0 + e) + jnp.log1p(-lb) * LOG2E
    g2 = jnp.maximum(a2, l2_lb) + jnp.log2(1.0 + jnp.exp2(-jnp.abs(a2 - l2_lb)))
    k = (1.0 - lb) * (jnp.where(zf > 0.0, e, 1.0) / (1.0 + e))
    f = 1.0 - k
    g_hi = g2.astype(BF16)
    r1 = g2 - g_hi.astype(F32)
    g_mid = r1.astype(BF16)
    g_lo = (r1 - g_mid.astype(F32)).astype(BF16)
    c2 = jnp.dot(tri3, jnp.concatenate([g_hi, g_mid, g_lo], axis=0), preferred_element_type=F32)
    c_slot[...] = c2
    return q, k, f, c2


def _scan_scores(q, k, f, c2, c_slot, tab_ref, *, reverse):
    nb = SCAN_BLOCK
    half = nb // 2
    n_tiles = nb // SUBLANES
    tile = lambda x, i: x[i * SUBLANES:(i + 1) * SUBLANES]

    def crow(r):
        return c_slot[r:r + 1, :]

    lvl = _dot_nt(q.astype(BF16), k.astype(BF16)) * tab_ref[_M_DIAG]
    qt = (q * f * tab_ref[_M_Q1]).astype(BF16)
    kt = (k * tab_ref[_M_K1]).astype(BF16)
    lvl = lvl + _dot_nt(qt, kt) * tab_ref[_M_G1]

    upper_half_tile = lax.broadcasted_iota(jnp.int32, (SUBLANES, LANES), 0) >= SUBLANES // 2
    for m, mq, mg in ((2, _M_Q2, _M_G2), (4, _M_Q4, _M_G4)):
        refs = []
        for t in range(n_tiles):
            r0 = t * SUBLANES + (m if reverse else m - 1)
            rb = jnp.broadcast_to(crow(r0), (SUBLANES, LANES))
            if m == 2:
                rb = jnp.where(upper_half_tile, jnp.broadcast_to(crow(r0 + 4), (SUBLANES, LANES)), rb)
            refs.append(rb)
        w = jnp.exp2(-jnp.abs(c2 - jnp.concatenate(refs, axis=0)))
        wq = w * tab_ref[mq]
        wk = w - wq
        lvl = lvl + _dot_nt((q * wq).astype(BF16), (k * wk).astype(BF16)) * tab_ref[mg]
    score_tiles = [tile(lvl, i) for i in range(n_tiles)]

    for m, mh in zip(_HIGH_LEVELS, (_M_H8, _M_H16, _M_H32, None)):
        q_parts, k_parts, q_tiles = [], [], []
        for g in range(nb // (2 * m)):
            base = g * 2 * m
            first, second = slice(base, base + m), slice(base + m, base + 2 * m)
            q_rows, k_rows = (first, second) if reverse else (second, first)
            c_ref = crow(base + m if reverse else base + m - 1)
            q_parts.append(q[q_rows] * jnp.exp2(c2[q_rows] - c_ref))
            k_part = k[k_rows] * jnp.exp2(c_ref - c2[k_rows])
            zeros = jnp.zeros((m, LANES), F32)
            k_parts += [zeros, k_part] if reverse else [k_part, zeros]
            q_tiles += list(range(q_rows.start // SUBLANES, q_rows.stop // SUBLANES))
        res = _dot_nt(jnp.concatenate(q_parts, axis=0).astype(BF16),
                      jnp.concatenate(k_parts, axis=0).astype(BF16))
        if mh is not None:
            res = res * tab_ref[mh, 0:half, :]
        for j, t in enumerate(q_tiles):
            score_tiles[t] = score_tiles[t] + tile(res, j)
    return jnp.concatenate(score_tiles, axis=0)


def _scan_kernel(*refs, reverse, finalize, n_sub, group):
    if finalize:
        (zq_ref, zv_ref, zf_ref, lb_ref, s0_ref, tab_ref, tri_ref, zg_ref, oo_ref, ng_ref,
         o_ref, sfin_ref, s_scr, c_scr) = refs
    else:
        zq_ref, zv_ref, zf_ref, lb_ref, s0_ref, tab_ref, tri_ref, o_ref, sfin_ref, s_scr, c_scr = refs
    nb = SCAN_BLOCK
    hps = SCAN_HEADS_PER_STEP
    ci = pl.program_id(2)

    @pl.when(ci == 0)
    def _():
        s_scr[...] = s0_ref[0]

    tri3 = tri_ref[...]
    states = [s_scr[hh] for hh in range(hps)]
    order = [(n_sub - 1 - i) if reverse else i for i in range(n_sub)]
    for g0 in range(0, n_sub, group):
        items = [(j, hh) for j in order[g0:g0 + group] for hh in range(hps)]
        where = {it: (slice(it[0] * nb, (it[0] + 1) * nb), slice(it[1] * LANES, (it[1] + 1) * LANES))
                 for it in items}
        slot = {it: c_scr.at[it[0] * hps + it[1]] for it in items}
        act, scores, sv, upd, dec, q_in = {}, {}, {}, {}, {}, {}
        for it in items:
            rows, cols = where[it]
            act[it] = _scan_gates(zq_ref[0, rows, cols], zf_ref[0, rows, cols], lb_ref[0, :, cols],
                                  slot[it], tri3)
        for it in items:
            q, k, f, c2 = act[it]
            scores[it] = _scan_scores(q, k, f, c2, slot[it], tab_ref, reverse=reverse)
        for it in items:
            rows, cols = where[it]
            q, k, f, c2 = act[it]
            v = zv_ref[0, rows, cols].astype(BF16)
            c_tot = slot[it][0:1, :] if reverse else slot[it][nb - 1:nb, :]
            sv[it] = jnp.dot(scores[it].astype(BF16), v, preferred_element_type=F32)
            upd[it] = _dot_tn(v, (k * jnp.exp2(c_tot - c2)).astype(BF16))
            dec[it] = jnp.exp2(c_tot)
            q_in[it] = (q * jnp.exp2(c2)).astype(BF16)
        for it in items:
            rows, cols = where[it]
            hh = it[1]
            o = sv[it] + _dot_nt(q_in[it], states[hh].astype(BF16))
            states[hh] = states[hh] * dec[it] + upd[it]
            if finalize:
                o = o + oo_ref[0, rows, cols]
                ms = jnp.mean(o * o, axis=-1, keepdims=True)
                o = o * lax.rsqrt(ms + RMS_EPS) * ng_ref[0, :, cols] * _silu(zg_ref[0, rows, cols])
            o_ref[0, rows, cols] = o.astype(o_ref.dtype)
    for hh in range(hps):
        s_scr[hh] = states[hh]

    @pl.when(ci == pl.num_programs(2) - 1)
    def _():
        sfin_ref[0] = s_scr[...]


def _hgrn2_scan(z, col_q, col_v, col_f, lb, s0, *, reverse, final=None):
    b, l, _ = z.shape
    hps = SCAN_HEADS_PER_STEP
    heads, hd, wd = A_HEADS, LANES, SCAN_HEADS_PER_STEP * LANES
    cb = min(l, 1024)
    nc = l // cb
    n_sub = cb // SCAN_BLOCK
    assert heads % hps == 0 and col_q % hps == 0 and col_v % hps == 0 and col_f % hps == 0

    def chunk(i):
        return (nc - 1 - i) if reverse else i

    def zspec(col0):
        return pl.BlockSpec((1, cb, wd), lambda b, h, i: (b, chunk(i), col0 // hps + h))

    def whole(a):
        return pl.BlockSpec(a.shape, lambda b, h, i: (0,) * a.ndim)

    head_row = pl.BlockSpec((1, 1, wd), lambda b, h, i: (h, 0, 0))
    state = pl.BlockSpec((1, hps, hd, hd), lambda b, h, i: (b, h, 0, 0))
    tabs, tri = _scan_constants(reverse)
    in_specs = [zspec(col_q), zspec(col_v), zspec(col_f), head_row, state, whole(tabs), whole(tri)]
    args = [z, z, z, lb.reshape(heads // hps, 1, wd), s0, tabs, tri]
    if final is not None:
        col_g, other, norm_g = final
        assert col_g % hps == 0
        in_specs += [zspec(col_g), zspec(0), head_row]
        args += [z, other, norm_g.reshape(heads // hps, 1, wd)]
    out_dtype = BF16 if final is not None else F32
    return pl.pallas_call(
        functools.partial(_scan_kernel, reverse=reverse, finalize=final is not None, n_sub=n_sub,
                          group=min(n_sub, SCAN_GROUP)),
        grid=(b, heads // hps, nc),
        in_specs=in_specs,
        out_specs=[zspec(0), state],
        out_shape=[jax.ShapeDtypeStruct((b, l, heads * hd), out_dtype),
                   jax.ShapeDtypeStruct((b, heads, hd, hd), F32)],
        scratch_shapes=[pltpu.VMEM((hps, hd, hd), F32), pltpu.VMEM((n_sub * hps, SCAN_BLOCK, hd), F32)],
        compiler_params=_cparams("parallel", "parallel", "arbitrary"),
        name="hgrn2_scan_bwd" if reverse else "hgrn2_scan_fwd",
    )(*args)


def _dft_cos_sin(n):
    idx = np.arange(n, dtype=np.int64)
    ang = 2.0 * np.pi * ((idx[:, None] * idx[None, :]) % n).astype(np.float64) / n
    return np.cos(ang), np.sin(ang)


def _fourier_two_stage_kernel(z_ref, csc_ref, csq_ref, csr_ref, tw1_ref, o_ref, tre_scr, tim_scr, *, q):
    nr = LANES
    csc = csc_ref[...]
    csq = csq_ref[...]
    tw_c1 = tw1_ref[0]
    tw_s1 = tw1_ref[1]

    def stage_a(r, tw):
        tw_c, tw_s = tw
        x = z_ref[0, pl.ds(r, q, stride=nr), :]
        uv = jnp.dot(x.astype(BF16), csc, preferred_element_type=F32)
        res = jnp.dot(csq, uv.astype(BF16), preferred_element_type=F32)
        cu, cv = res[:q, :LANES], res[:q, LANES:]
        su, sv = res[q:, :LANES], res[q:, LANES:]
        g_re = cu - sv
        g_im = -(cv + su)
        tre_scr[pl.ds(r, q, stride=nr), :] = g_re * tw_c + g_im * tw_s
        tim_scr[pl.ds(r, q, stride=nr), :] = g_im * tw_c - g_re * tw_s
        return tw_c * tw_c1 - tw_s * tw_s1, tw_s * tw_c1 + tw_c * tw_s1

    lax.fori_loop(0, nr, stage_a, (jnp.ones((q, LANES), F32), jnp.zeros((q, LANES), F32)), unroll=4)

    csr = csr_ref[...]

    def stage_b(ka, carry):
        off = pl.multiple_of(ka * nr, nr)
        t = jnp.concatenate([tre_scr[pl.ds(off, nr), :], tim_scr[pl.ds(off, nr), :]], axis=0)
        y = jnp.dot(csr, t.astype(BF16), preferred_element_type=F32)
        o_ref[0, pl.ds(ka, nr, stride=q), :] = y
        return carry

    lax.fori_loop(0, q, stage_b, 0, unroll=4)


def _fourier_direct_kernel(z_ref, csc_ref, csl_ref, o_ref):
    uv = jnp.dot(z_ref[0].astype(BF16), csc_ref[...], preferred_element_type=F32)
    t = jnp.concatenate([uv[:, :LANES], uv[:, LANES:]], axis=0)
    o_ref[0] = jnp.dot(csl_ref[...], t.astype(BF16), preferred_element_type=F32)


def _fourier_mixer(z, col0):
    b, l, _ = z.shape
    scale = 1.0 / math.sqrt(l * LANES)
    cc, sc = _dft_cos_sin(LANES)
    csc = jnp.asarray(np.concatenate([cc, sc], axis=1) * scale, BF16)
    out_shape = jax.ShapeDtypeStruct((b, l, F_GROUPS * LANES), F32)
    slab = pl.BlockSpec((1, l, LANES), lambda b, g: (b, 0, col0 + g))
    out_slab = pl.BlockSpec((1, l, LANES), lambda b, g: (b, 0, g))

    def whole(a):
        return pl.BlockSpec(a.shape, lambda b, g: (0,) * a.ndim)

    if l <= 1024:
        cl, sl = _dft_cos_sin(l)
        csl = jnp.asarray(np.concatenate([cl, -sl], axis=1), BF16)
        return pl.pallas_call(
            _fourier_direct_kernel,
            grid=(b, F_GROUPS),
            in_specs=[slab, whole(csc), whole(csl)],
            out_specs=out_slab,
            out_shape=out_shape,
            compiler_params=_cparams("parallel", "parallel"),
            name="fourier_direct",
        )(z, csc, csl)

    q = l // LANES
    cq, sq = _dft_cos_sin(q)
    csq = jnp.asarray(np.concatenate([cq, sq], axis=0), BF16)
    csr = jnp.asarray(np.concatenate([cc, sc], axis=1), BF16)
    ang1 = 2.0 * np.pi * np.arange(q, dtype=np.float64) / l
    tw1 = jnp.asarray(np.stack([np.broadcast_to(np.cos(ang1)[:, None], (q, LANES)),
                                np.broadcast_to(np.sin(ang1)[:, None], (q, LANES))]), F32)
    return pl.pallas_call(
        functools.partial(_fourier_two_stage_kernel, q=q),
        grid=(b, F_GROUPS),
        in_specs=[pl.BlockSpec((1, l, LANES), lambda b, g: (b, 0, col0 + g), pipeline_mode=pl.Buffered(1)),
                  whole(csc), whole(csq), whole(csr), whole(tw1)],
        out_specs=pl.BlockSpec((1, l, LANES), lambda b, g: (b, 0, g), pipeline_mode=pl.Buffered(1)),
        out_shape=out_shape,
        scratch_shapes=[pltpu.VMEM((l, LANES), F32), pltpu.VMEM((l, LANES), F32)],
        compiler_params=_cparams("parallel", "parallel"),
        name="fourier_two_stage",
    )(z, csc, csq, csr, tw1)


def _gelu(x):
    return jax.nn.gelu(x, approximate=True)


def _chunk_mlp_kernel(zu_ref, zv_ref, g_ref, b_ref, ws_ref, bs_ref, o_ref, *, n_chunks):
    v = _layer_norm(_gelu(zv_ref[0])) * g_ref[...] + b_ref[...]
    vb = v.astype(BF16)
    for c in range(n_chunks):
        rows = slice(c * MLP_CHUNK, (c + 1) * MLP_CHUNK)
        for g in range(C_GROUPS):
            cols = slice(g * LANES, (g + 1) * LANES)
            sv = jnp.dot(ws_ref[g], vb[rows, cols], preferred_element_type=F32) + bs_ref[g]
            o_ref[0, rows, cols] = (_gelu(zu_ref[0, rows, cols]) * sv).astype(o_ref.dtype)


def _chunk_mlp_mixer(z, col_u, col_v, ln_g, ln_b, w_s, b_s):
    b, l, _ = z.shape
    dc = C_GROUPS * LANES
    tm = min(l, 512)
    bs_full = jnp.broadcast_to(b_s[:, :, None], (C_GROUPS, MLP_CHUNK, LANES)).astype(F32)

    def whole(a):
        return pl.BlockSpec(a.shape, lambda b, i: (0,) * a.ndim)

    ws = w_s.astype(BF16)
    g2, b2 = ln_g.reshape(1, dc), ln_b.reshape(1, dc)
    return pl.pallas_call(
        functools.partial(_chunk_mlp_kernel, n_chunks=tm // MLP_CHUNK),
        grid=(b, l // tm),
        in_specs=[pl.BlockSpec((1, tm, dc), lambda b, i: (b, i, col_u)),
                  pl.BlockSpec((1, tm, dc), lambda b, i: (b, i, col_v)),
                  whole(g2), whole(b2), whole(ws), whole(bs_full)],
        out_specs=pl.BlockSpec((1, tm, dc), lambda b, i: (b, i, 0)),
        out_shape=jax.ShapeDtypeStruct((b, l, dc), BF16),
        compiler_params=_cparams("parallel", "parallel"),
        name="chunk_mlp_mixer",
    )(z, z, g2, b2, ws, bs_full)


OUTPROJ_SUBTILE = 128


def _outproj_kernel(oa_ref, yf_ref, yc_ref, w_ref, x_ref, gate_ref, lng_ref, lnb_ref,
                    sh_ref, sc_ref, x1_ref, h_ref, *, alpha):
    tm = x_ref.shape[1]
    sub = min(tm, OUTPROJ_SUBTILE)
    for r0 in range(0, tm, sub):
        rows = slice(r0, r0 + sub)
        mixed = jnp.concatenate([oa_ref[0, rows, :], yf_ref[0, rows, :].astype(BF16), yc_ref[0, rows, :]], axis=1)
        y = jnp.dot(mixed, w_ref[...], preferred_element_type=F32)
        x1 = _layer_norm(alpha * x_ref[0, rows, :] + gate_ref[0] * y) * lng_ref[...] + lnb_ref[...]
        x1_ref[0, rows, :] = x1
        h_ref[0, rows, :] = (_layer_norm(x1) * (1.0 + sc_ref[0]) + sh_ref[0]).astype(BF16)


def _out_projection(o_a, y_f, y_c, w_out, x, gate, ln_g, ln_b, shift2, scale2, alpha):
    b, l, d = x.shape
    da, df, dc = o_a.shape[-1], y_f.shape[-1], y_c.shape[-1]
    tm = min(l, 512)
    row = lambda w: pl.BlockSpec((1, tm, w), lambda b, i: (b, i, 0))

    wspec = pl.BlockSpec((da + df + dc, d), lambda b, i: (0, 0), pipeline_mode=pl.Buffered(1))
    vec = pl.BlockSpec((1, d), lambda b, i: (0, 0))
    return pl.pallas_call(
        functools.partial(_outproj_kernel, alpha=alpha),
        grid=(b, l // tm),
        in_specs=[row(da), row(df), row(dc),
                  wspec,
                  row(d), _cond_spec(gate), vec, vec, _cond_spec(shift2), _cond_spec(scale2)],
        out_specs=[row(d), row(d)],
        out_shape=[jax.ShapeDtypeStruct((b, l, d), F32), jax.ShapeDtypeStruct((b, l, d), BF16)],
        compiler_params=_cparams("parallel", "parallel"),
        name="out_projection",
    )(o_a, y_f, y_c, w_out, x, gate, ln_g.reshape(1, d), ln_b.reshape(1, d), shift2, scale2)


def _ffn_up_kernel(*refs, width, vertical, tm):
    if vertical:
        h_ref, hp_ref, hn_ref, wg_ref, wu_ref, cw_ref, cb_ref, o_ref, g_scr = refs
    else:
        h_ref, wg_ref, wu_ref, cw_ref, cb_ref, o_ref, g_scr = refs
    i = pl.program_id(1)
    wg = wg_ref[...]
    h = h_ref[0]
    halo = width if vertical else 0
    g_scr[halo:halo + tm, :] = jnp.dot(h, wg, preferred_element_type=F32)
    if vertical:
        not_first = (i > 0).astype(F32)
        not_last = (i < pl.num_programs(1) - 1).astype(F32)
        g_scr[0:halo, :] = jnp.dot(hp_ref[0], wg, preferred_element_type=F32) * not_first
        g_scr[halo + tm:, :] = jnp.dot(hn_ref[0], wg, preferred_element_type=F32) * not_last
    ext = g_scr[...]
    n_ext = ext.shape[0]
    colpos = lax.broadcasted_iota(jnp.int32, ext.shape, 0) % width
    left = jnp.where(colpos == 0, 0.0, pltpu.roll(ext, 1, axis=0))
    right = jnp.where(colpos == width - 1, 0.0, pltpu.roll(ext, n_ext - 1, axis=0))
    taps = (left, ext, right)
    acc = None
    for dh in (range(3) if vertical else (1,)):
        base = halo + (dh - 1) * width
        for dw in range(3):
            term = taps[dw][base:base + tm, :] * cw_ref[dh * 3 + dw:dh * 3 + dw + 1, :]
            acc = term if acc is None else acc + term
    gate = _silu(acc + cb_ref[...])
    up = jnp.dot(h, wu_ref[...], preferred_element_type=F32)
    o_ref[0] = (gate * up).astype(o_ref.dtype)


def _ffn_up(h, w_gate, w_up, conv_w, conv_b, width):
    b, l, d = h.shape
    f = w_gate.shape[1]
    vertical = l > width
    tm = min(l, 1024)
    tf = 512
    assert tm % width == 0 and f % tf == 0
    rows_per_tile = tm // width
    n_rows = l // width
    in_specs = [pl.BlockSpec((1, tm, d), lambda b, i, j: (b, i, 0))]
    args = [h]
    if vertical:
        in_specs += [
            pl.BlockSpec((1, width, d), lambda b, i, j: (b, jnp.maximum(i * rows_per_tile - 1, 0), 0)),
            pl.BlockSpec((1, width, d), lambda b, i, j: (b, jnp.minimum((i + 1) * rows_per_tile, n_rows - 1), 0)),
        ]
        args += [h, h]
    wspec = pl.BlockSpec((d, tf), lambda b, i, j: (0, j))
    in_specs += [wspec, wspec,
                 pl.BlockSpec((9, tf), lambda b, i, j: (0, j)),
                 pl.BlockSpec((1, tf), lambda b, i, j: (0, j))]
    args += [w_gate, w_up, conv_w.reshape(9, f), conv_b.reshape(1, f)]
    ext_rows = tm + (2 * width if vertical else 0)
    return pl.pallas_call(
        functools.partial(_ffn_up_kernel, width=width, vertical=vertical, tm=tm),
        grid=(b, l // tm, f // tf),
        in_specs=in_specs,
        out_specs=pl.BlockSpec((1, tm, tf), lambda b, i, j: (b, i, j)),
        out_shape=jax.ShapeDtypeStruct((b, l, f), BF16),
        scratch_shapes=[pltpu.VMEM((ext_rows, tf), F32)],
        compiler_params=_cparams("parallel", "parallel", "arbitrary"),
        name="ffn_up",
    )(*args)


def _ffn_down_kernel(hid_ref, w_ref, x_ref, gate_ref, lng_ref, lnb_ref, o_ref, *, alpha):
    f = jnp.dot(hid_ref[0], w_ref[...], preferred_element_type=F32)
    y = alpha * x_ref[0] + gate_ref[0] * f
    o_ref[0] = _layer_norm(y) * lng_ref[...] + lnb_ref[...]


def _ffn_down(hid, w_down, x, gate, ln_g, ln_b, alpha):
    b, l, f = hid.shape
    d = x.shape[-1]
    tm = min(l, 256)
    vec = pl.BlockSpec((1, d), lambda b, i: (0, 0))
    return pl.pallas_call(
        functools.partial(_ffn_down_kernel, alpha=alpha),
        grid=(b, l // tm),
        in_specs=[pl.BlockSpec((1, tm, f), lambda b, i: (b, i, 0)),
                  pl.BlockSpec((f, d), lambda b, i: (0, 0), pipeline_mode=pl.Buffered(1)),
                  pl.BlockSpec((1, tm, d), lambda b, i: (b, i, 0)),
                  _cond_spec(gate), vec, vec],
        out_specs=pl.BlockSpec((1, tm, d), lambda b, i: (b, i, 0)),
        out_shape=jax.ShapeDtypeStruct((b, l, d), F32),
        compiler_params=_cparams("parallel", "parallel"),
        name="ffn_down",
    )(hid, w_down, x, gate, ln_g.reshape(1, d), ln_b.reshape(1, d))


def kernel(x, c, ctx, c_ctx, w_ada, b_ada, w_in, lower_bounds, a_norm_g, sg_norm_g, sg_norm_b,
           w_spatial, b_spatial, w_out, ln1_g, ln1_b, w_gate, w_up, conv_w, conv_b, w_down, ln2_g, ln2_b):
    batch, seq, d = x.shape
    depth = w_in.shape[0]
    d_a = a_norm_g.shape[-1]
    d_c = sg_norm_g.shape[-1]
    d_f = w_in.shape[-1] - 5 * d_a - 2 * d_c
    assert d_a == A_HEADS * LANES and d_c == C_GROUPS * LANES and d_f == F_GROUPS * LANES
    alpha = (2 * depth) ** 0.25
    a_blk = d_a // LANES

    lbs = jnp.cumsum(jax.nn.softmax(lower_bounds.astype(F32), axis=1), axis=1)
    lbs = lbs - lbs[:, :1]

    rows = -(-(batch + 1) // SUBLANES) * SUBLANES
    cond = jnp.zeros((rows, d), F32).at[:batch].set(c).at[batch].set(c_ctx)
    mods = _ada_modulation(cond, w_ada, b_ada)

    w_in_b, w_out_b = w_in.astype(BF16), w_out.astype(BF16)
    w_gate_b, w_up_b, w_down_b = w_gate.astype(BF16), w_up.astype(BF16), w_down.astype(BF16)
    zero_state = jnp.zeros((batch, A_HEADS, LANES, LANES), F32)

    def mixers(z, s_fwd, s_bwd, l, with_output=True):
        o_b, sf_b = _hgrn2_scan(z, 0, a_blk, 3 * a_blk, lbs[1, l], s_bwd, reverse=True)
        if not with_output:
            _, sf_f = _hgrn2_scan(z, 0, a_blk, 2 * a_blk, lbs[0, l], s_fwd, reverse=False)
            return None, sf_f, sf_b
        o_a, sf_f = _hgrn2_scan(z, 0, a_blk, 2 * a_blk, lbs[0, l], s_fwd, reverse=False,
                                final=(4 * a_blk, o_b, a_norm_g[l]))
        y_f = _fourier_mixer(z, 5 * a_blk)
        cu = (5 * d_a + d_f) // d_c
        y_c = _chunk_mlp_mixer(z, cu, cu + 1, sg_norm_g[l], sg_norm_b[l], w_spatial[l], b_spatial[l])
        return (o_a, y_f, y_c), sf_f, sf_b

    def rest_of_layer(xs, ys, m, l, width):
        sh2, sc2 = m[3], m[4]
        x1, h2 = _out_projection(*ys, w_out_b[l], xs, m[2], ln1_g[l], ln1_b[l], sh2, sc2, alpha)
        hid = _ffn_up(h2, w_gate_b[l], w_up_b[l], conv_w[l], conv_b[l], width)
        return _ffn_down(hid, w_down_b[l], x1, m[5], ln2_g[l], ln2_b[l], alpha)

    for l in range(depth):
        last = l == depth - 1
        m_all = [mods[l, :, i * d:(i + 1) * d] for i in range(6)]
        m_x = [t[:batch].reshape(batch, 1, d) for t in m_all]
        m_c = [t[batch:batch + 1].reshape(1, 1, d) for t in m_all]

        zc = _in_projection(ctx, m_c[0], m_c[1], w_in_b[l])
        ys_c, s_f, s_b = mixers(zc, zero_state, zero_state, l, with_output=not last)
        zx = _in_projection(x, m_x[0], m_x[1], w_in_b[l])
        ys_x, _, _ = mixers(zx, s_f, s_b, l)
        x = rest_of_layer(x, ys_x, m_x, l, GRID_W)
        if not last:
            ctx = rest_of_layer(ctx, ys_c, m_c, l, ctx.shape[1])
    return x
```

```python
import functools
import math

import numpy as np
import jax
import jax.numpy as jnp
from jax import lax
from jax.experimental import pallas as pl
from jax.experimental.pallas import tpu as pltpu

F32 = jnp.float32
BF16 = jnp.bfloat16

LANES = 128
SUBLANES = 8
VMEM_LIMIT_BYTES = 56 * 1024 * 1024

A_HEADS = 8
F_GROUPS = 4
C_GROUPS = 4
MLP_CHUNK = 128
SCAN_BLOCK = 128
GRID_W = 64
LN_EPS = 1e-6
RMS_EPS = 1e-6


def _cparams(*sem):
    return pltpu.CompilerParams(dimension_semantics=sem, vmem_limit_bytes=VMEM_LIMIT_BYTES)


def _silu(x):
    return x * jax.nn.sigmoid(x)


def _layer_norm(x):
    mu = jnp.mean(x, axis=-1, keepdims=True)
    xc = x - mu
    var = jnp.mean(xc * xc, axis=-1, keepdims=True)
    return xc * lax.rsqrt(var + LN_EPS)


def _pick_tile(n, cap, quantum=LANES):
    return max(t for t in range(quantum, cap + 1, quantum) if n % t == 0)


def _cond_spec(arr):
    d = arr.shape[-1]
    if arr.shape[0] == 1:
        return pl.BlockSpec((1, 1, d), lambda b, *_: (0, 0, 0))
    return pl.BlockSpec((1, 1, d), lambda b, *_: (b, 0, 0))


def _ada_kernel(cond_ref, w_ref, b_ref, o_ref):
    s = _silu(cond_ref[...]).astype(BF16)
    o_ref[0] = jnp.dot(s, w_ref[0].astype(BF16), preferred_element_type=F32) + b_ref[0]


def _ada_modulation(cond, w_ada, b_ada):
    depth, d, n = w_ada.shape
    rows = cond.shape[0]
    tn = 1024
    return pl.pallas_call(
        _ada_kernel,
        grid=(depth, n // tn),
        in_specs=[
            pl.BlockSpec((rows, d), lambda l, j: (0, 0)),
            pl.BlockSpec((1, d, tn), lambda l, j: (l, 0, j)),
            pl.BlockSpec((1, 1, tn), lambda l, j: (l, 0, j)),
        ],
        out_specs=pl.BlockSpec((1, rows, tn), lambda l, j: (l, 0, j)),
        out_shape=jax.ShapeDtypeStruct((depth, rows, n), F32),
        compiler_params=_cparams("parallel", "parallel"),
        name="ada_modulation",
    )(cond, w_ada, b_ada.reshape(depth, 1, n))


def _inproj_kernel(x_ref, sh_ref, sc_ref, w_ref, o_ref, h_scr):
    @pl.when(pl.program_id(2) == 0)
    def _():
        h = _layer_norm(x_ref[0]) * (1.0 + sc_ref[0]) + sh_ref[0]
        h_scr[...] = h.astype(BF16)

    o_ref[0] = jnp.dot(h_scr[...], w_ref[...], preferred_element_type=F32)


def _in_projection(x, shift, scale, w):
    b, l, d = x.shape
    n = w.shape[1]
    tm = min(l, 1024)
    tn = _pick_tile(n, 1792)
    return pl.pallas_call(
        _inproj_kernel,
        grid=(b, l // tm, n // tn),
        in_specs=[
            pl.BlockSpec((1, tm, d), lambda b, i, j: (b, i, 0)),
            _cond_spec(shift),
            _cond_spec(scale),
            pl.BlockSpec((d, tn), lambda b, i, j: (0, j)),
        ],
        out_specs=pl.BlockSpec((1, tm, tn), lambda b, i, j: (b, i, j)),
        out_shape=jax.ShapeDtypeStruct((b, l, n), F32),
        scratch_shapes=[pltpu.VMEM((tm, d), BF16)],
        compiler_params=_cparams("parallel", "parallel", "arbitrary"),
        name="in_projection",
    )(x, shift, scale, w)


LOG2E = 1.4426950408889634
SCAN_HEADS_PER_STEP = 2
SCAN_STAGE_LAGS = (2, 2)
_HIGH_LEVELS = (8, 16, 32, 64)
_M_DIAG, _M_G1, _M_G2, _M_G4, _M_Q1, _M_K1, _M_Q2, _M_Q4, _M_H8, _M_H16, _M_H32 = range(11)
_R_Q1, _R_K1, _R_Q2, _R_Q4 = range(4)


def _scan_constants(reverse):
    nb = SCAN_BLOCK
    row = np.arange(nb)[:, None]
    col = np.arange(nb)[None, :]
    x = row ^ col
    tabs = np.zeros((11, nb, nb), np.float32)
    tabs[_M_DIAG] = x == 0
    tabs[_M_G1], tabs[_M_G2], tabs[_M_G4] = x < 2, x < 4, x < 8

    def q_role(m):
        upper = (row & m) == 0
        return np.broadcast_to(upper if reverse else ~upper, (nb, nb))

    tabs[_M_Q1], tabs[_M_Q2], tabs[_M_Q4] = q_role(1), q_role(2), q_role(4)
    tabs[_M_K1] = 1.0 - tabs[_M_Q1]
    for idx, m in ((_M_H8, 8), (_M_H16, 16), (_M_H32, 32)):
        grp = np.arange(nb // 2)[:, None] // m
        tabs[idx, :nb // 2] = (col // (2 * m)) == grp
    tri = (col >= row) if reverse else (col <= row)
    roles = jnp.asarray(tabs[[_M_Q1, _M_K1, _M_Q2, _M_Q4]], BF16)
    return jnp.asarray(tabs), roles, jnp.asarray(np.concatenate([tri, tri, tri], axis=1), BF16)


def _dot_nt(a, b):
    return lax.dot_general(a, b, (((1,), (1,)), ((), ())), preferred_element_type=F32)


def _dot_tn(a, b):
    return lax.dot_general(a, b, (((0,), (0,)), ((), ())), preferred_element_type=F32)


def _scan_gates(zq, zf, lb, c_slot, tri3):
    q = _silu(zq)
    l2_lb = jnp.log2(lb)
    e = jnp.exp2(-LOG2E * jnp.abs(zf))
    a2 = jnp.minimum(zf, 0.0) * LOG2E - jnp.log2(1.0 + e) + jnp.log1p(-lb) * LOG2E
    g2 = jnp.maximum(a2, l2_lb) + jnp.log2(1.0 + jnp.exp2(-jnp.abs(a2 - l2_lb)))
    k = (1.0 - lb) * (jnp.where(zf > 0.0, e, 1.0) / (1.0 + e))
    f = 1.0 - k
    g_hi = g2.astype(BF16)
    r1 = g2 - g_hi.astype(F32)
    g_mid = r1.astype(BF16)
    g_lo = (r1 - g_mid.astype(F32)).astype(BF16)
    c2 = jnp.dot(tri3, jnp.concatenate([g_hi, g_mid, g_lo], axis=0), preferred_element_type=F32)
    c_slot[...] = c2
    return q, k, f, c2


def _scan_scores(q, k, f, c2, c_slot, tab_ref, role_ref, *, reverse):
    nb = SCAN_BLOCK
    half = nb // 2
    n_tiles = nb // SUBLANES
    tile = lambda x, i: x[i * SUBLANES:(i + 1) * SUBLANES]
    qb, kb = q.astype(BF16), k.astype(BF16)

    def crow(r):
        return c_slot[r:r + 1, :]

    lvl = _dot_nt(qb, kb) * tab_ref[_M_DIAG]
    qt = qb * (f.astype(BF16) * role_ref[_R_Q1])
    kt = kb * role_ref[_R_K1]
    lvl = lvl + _dot_nt(qt, kt) * tab_ref[_M_G1]

    upper_half_tile = lax.broadcasted_iota(jnp.int32, (SUBLANES, LANES), 0) >= SUBLANES // 2
    for m, rq, mg in ((2, _R_Q2, _M_G2), (4, _R_Q4, _M_G4)):
        refs = []
        for t in range(n_tiles):
            r0 = t * SUBLANES + (m if reverse else m - 1)
            rb = jnp.broadcast_to(crow(r0), (SUBLANES, LANES))
            if m == 2:
                rb = jnp.where(upper_half_tile, jnp.broadcast_to(crow(r0 + 4), (SUBLANES, LANES)), rb)
            refs.append(rb)
        w = jnp.exp2(-jnp.abs(c2 - jnp.concatenate(refs, axis=0))).astype(BF16)
        wq = w * role_ref[rq]
        wk = w - wq
        lvl = lvl + _dot_nt(qb * wq, kb * wk) * tab_ref[mg]
    score_tiles = [tile(lvl, i) for i in range(n_tiles)]

    for m, mh in zip(_HIGH_LEVELS, (_M_H8, _M_H16, _M_H32, None)):
        packed = m % (2 * SUBLANES) == 0
        q_parts, k_parts, q_tiles = [], [], []
        for g in range(nb // (2 * m)):
            base = g * 2 * m
            first, second = slice(base, base + m), slice(base + m, base + 2 * m)
            q_rows, k_rows = (first, second) if reverse else (second, first)
            c_ref = crow(base + m if reverse else base + m - 1)
            wq = jnp.exp2(c2[q_rows] - c_ref)
            wk = jnp.exp2(c_ref - c2[k_rows])
            if packed:
                q_parts.append(qb[q_rows] * wq.astype(BF16))
                k_part = kb[k_rows] * wk.astype(BF16)
                zeros = jnp.zeros((m, LANES), BF16)
            else:
                q_parts.append(q[q_rows] * wq)
                k_part = k[k_rows] * wk
                zeros = jnp.zeros((m, LANES), F32)
            k_parts += [zeros, k_part] if reverse else [k_part, zeros]
            q_tiles += list(range(q_rows.start // SUBLANES, q_rows.stop // SUBLANES))
        res = _dot_nt(jnp.concatenate(q_parts, axis=0).astype(BF16),
                      jnp.concatenate(k_parts, axis=0).astype(BF16))
        if mh is not None:
            res = res * tab_ref[mh, 0:half, :]
        for j, t in enumerate(q_tiles):
            score_tiles[t] = score_tiles[t] + tile(res, j)
    return jnp.concatenate(score_tiles, axis=0), qb, kb


def _scan_kernel(*refs, reverse, finalize, n_sub):
    if finalize:
        (zq_ref, zv_ref, zf_ref, lb_ref, s0_ref, tab_ref, role_ref, tri_ref, zg_ref, oo_ref, ng_ref,
         o_ref, sfin_ref, s_scr, c_scr) = refs
    else:
        (zq_ref, zv_ref, zf_ref, lb_ref, s0_ref, tab_ref, role_ref, tri_ref,
         o_ref, sfin_ref, s_scr, c_scr) = refs
    nb = SCAN_BLOCK
    hps = SCAN_HEADS_PER_STEP
    ci = pl.program_id(2)

    @pl.when(ci == 0)
    def _():
        s_scr[...] = s0_ref[0]

    tri3 = tri_ref[...]
    states = [s_scr[hh] for hh in range(hps)]
    order = [(n_sub - 1 - i) if reverse else i for i in range(n_sub)]
    items = [(j, hh) for j in order for hh in range(hps)]
    where = {it: (slice(it[0] * nb, (it[0] + 1) * nb), slice(it[1] * LANES, (it[1] + 1) * LANES)) for it in items}
    slot = {it: c_scr.at[it[0] * hps + it[1]] for it in items}
    act, scores, packed = {}, {}, {}

    def stage_gates(it):
        rows, cols = where[it]
        act[it] = _scan_gates(zq_ref[0, rows, cols], zf_ref[0, rows, cols], lb_ref[0, :, cols], slot[it], tri3)

    def stage_scores(it):
        q, k, f, c2 = act[it]
        scores[it], *packed[it] = _scan_scores(q, k, f, c2, slot[it], tab_ref, role_ref, reverse=reverse)

    def stage_output(it):
        rows, cols = where[it]
        hh = it[1]
        _, _, _, c2 = act.pop(it)
        qb, kb = packed.pop(it)
        v = zv_ref[0, rows, cols].astype(BF16)
        c_tot = slot[it][0:1, :] if reverse else slot[it][nb - 1:nb, :]
        o = jnp.dot(scores.pop(it).astype(BF16), v, preferred_element_type=F32)
        o = o + _dot_nt(qb * jnp.exp2(c2).astype(BF16), states[hh].astype(BF16))
        states[hh] = states[hh] * jnp.exp2(c_tot) + _dot_tn(v, kb * jnp.exp2(c_tot - c2).astype(BF16))
        if finalize:
            o = o + oo_ref[0, rows, cols]
            ms = jnp.mean(o * o, axis=-1, keepdims=True)
            o = o * lax.rsqrt(ms + RMS_EPS) * ng_ref[0, :, cols] * _silu(zg_ref[0, rows, cols])
        o_ref[0, rows, cols] = o.astype(o_ref.dtype)

    lag_s, lag_o = SCAN_STAGE_LAGS
    for n in range(len(items) + lag_s + lag_o):
        if n < len(items):
            stage_gates(items[n])
        if 0 <= n - lag_s < len(items):
            stage_scores(items[n - lag_s])
        if 0 <= n - lag_s - lag_o < len(items):
            stage_output(items[n - lag_s - lag_o])
    for hh in range(hps):
        s_scr[hh] = states[hh]

    @pl.when(ci == pl.num_programs(2) - 1)
    def _():
        sfin_ref[0] = s_scr[...]


def _hgrn2_scan(z, col_q, col_v, col_f, lb, s0, *, reverse, final=None):
    b, l, _ = z.shape
    hps = SCAN_HEADS_PER_STEP
    heads, hd, wd = A_HEADS, LANES, SCAN_HEADS_PER_STEP * LANES
    cb = min(l, 1024)
    nc = l // cb
    n_sub = cb // SCAN_BLOCK
    assert heads % hps == 0 and col_q % hps == 0 and col_v % hps == 0 and col_f % hps == 0

    def chunk(i):
        return (nc - 1 - i) if reverse else i

    def zspec(col0):
        return pl.BlockSpec((1, cb, wd), lambda b, h, i: (b, chunk(i), col0 // hps + h))

    def whole(a):
        return pl.BlockSpec(a.shape, lambda b, h, i: (0,) * a.ndim)

    head_row = pl.BlockSpec((1, 1, wd), lambda b, h, i: (h, 0, 0))
    state = pl.BlockSpec((1, hps, hd, hd), lambda b, h, i: (b, h, 0, 0))
    tabs, roles, tri = _scan_constants(reverse)
    in_specs = [zspec(col_q), zspec(col_v), zspec(col_f), head_row, state, whole(tabs), whole(roles), whole(tri)]
    args = [z, z, z, lb.reshape(heads // hps, 1, wd), s0, tabs, roles, tri]
    if final is not None:
        col_g, other, norm_g = final
        assert col_g % hps == 0
        in_specs += [zspec(col_g), zspec(0), head_row]
        args += [z, other, norm_g.reshape(heads // hps, 1, wd)]
    out_dtype = BF16 if final is not None else F32
    return pl.pallas_call(
        functools.partial(_scan_kernel, reverse=reverse, finalize=final is not None, n_sub=n_sub),
        grid=(b, heads // hps, nc),
        in_specs=in_specs,
        out_specs=[zspec(0), state],
        out_shape=[jax.ShapeDtypeStruct((b, l, heads * hd), out_dtype),
                   jax.ShapeDtypeStruct((b, heads, hd, hd), F32)],
        scratch_shapes=[pltpu.VMEM((hps, hd, hd), F32), pltpu.VMEM((n_sub * hps, SCAN_BLOCK, hd), F32)],
        compiler_params=_cparams("parallel", "parallel", "arbitrary"),
        name="hgrn2_scan_bwd" if reverse else "hgrn2_scan_fwd",
    )(*args)


def _dft_cos_sin(n):
    idx = np.arange(n, dtype=np.int64)
    ang = 2.0 * np.pi * ((idx[:, None] * idx[None, :]) % n).astype(np.float64) / n
    return np.cos(ang), np.sin(ang)


FOURIER_PITCH = LANES + SUBLANES


def _fourier_two_stage_kernel(z_ref, csc_ref, csq_ref, csr_ref, tw1_ref, o_ref, tre_scr, tim_scr, *, q):
    nr = LANES
    pitch = FOURIER_PITCH
    csc = csc_ref[...]
    csq = csq_ref[...]
    tw_c1 = tw1_ref[0]
    tw_s1 = tw1_ref[1]

    def stage_a(r, tw):
        tw_c, tw_s = tw
        x = z_ref[0, pl.ds(r, q, stride=nr), :]
        uv = jnp.dot(x.astype(BF16), csc, preferred_element_type=F32)
        res = jnp.dot(csq, uv.astype(BF16), preferred_element_type=F32)
        cu, cv = res[:q, :LANES], res[:q, LANES:]
        su, sv = res[q:, :LANES], res[q:, LANES:]
        g_re = cu - sv
        g_im = -(cv + su)
        tre_scr[pl.ds(r, q, stride=pitch), :] = g_re * tw_c + g_im * tw_s
        tim_scr[pl.ds(r, q, stride=pitch), :] = g_im * tw_c - g_re * tw_s
        return tw_c * tw_c1 - tw_s * tw_s1, tw_s * tw_c1 + tw_c * tw_s1

    lax.fori_loop(0, nr, stage_a, (jnp.ones((q, LANES), F32), jnp.zeros((q, LANES), F32)), unroll=4)

    csr = csr_ref[...]

    def stage_b(ka, carry):
        off = pl.multiple_of(ka * pitch, SUBLANES)
        t = jnp.concatenate([tre_scr[pl.ds(off, nr), :], tim_scr[pl.ds(off, nr), :]], axis=0)
        y = jnp.dot(csr, t.astype(BF16), preferred_element_type=F32)
        o_ref[0, pl.ds(ka, nr, stride=q), :] = y
        return carry

    lax.fori_loop(0, q, stage_b, 0, unroll=4)


def _fourier_direct_kernel(z_ref, csc_ref, csl_ref, o_ref):
    uv = jnp.dot(z_ref[0].astype(BF16), csc_ref[...], preferred_element_type=F32)
    t = jnp.concatenate([uv[:, :LANES], uv[:, LANES:]], axis=0)
    o_ref[0] = jnp.dot(csl_ref[...], t.astype(BF16), preferred_element_type=F32)


def _fourier_mixer(z, col0):
    b, l, _ = z.shape
    scale = 1.0 / math.sqrt(l * LANES)
    cc, sc = _dft_cos_sin(LANES)
    csc = jnp.asarray(np.concatenate([cc, sc], axis=1) * scale, BF16)
    out_shape = jax.ShapeDtypeStruct((b, l, F_GROUPS * LANES), F32)
    slab = pl.BlockSpec((1, l, LANES), lambda b, g: (b, 0, col0 + g))
    out_slab = pl.BlockSpec((1, l, LANES), lambda b, g: (b, 0, g))

    def whole(a):
        return pl.BlockSpec(a.shape, lambda b, g: (0,) * a.ndim)

    if l <= 1024:
        cl, sl = _dft_cos_sin(l)
        csl = jnp.asarray(np.concatenate([cl, -sl], axis=1), BF16)
        return pl.pallas_call(
            _fourier_direct_kernel,
            grid=(b, F_GROUPS),
            in_specs=[slab, whole(csc), whole(csl)],
            out_specs=out_slab,
            out_shape=out_shape,
            compiler_params=_cparams("parallel", "parallel"),
            name="fourier_direct",
        )(z, csc, csl)

    q = l // LANES
    cq, sq = _dft_cos_sin(q)
    csq = jnp.asarray(np.concatenate([cq, sq], axis=0), BF16)
    csr = jnp.asarray(np.concatenate([cc, sc], axis=1), BF16)
    ang1 = 2.0 * np.pi * np.arange(q, dtype=np.float64) / l
    tw1 = jnp.asarray(np.stack([np.broadcast_to(np.cos(ang1)[:, None], (q, LANES)),
                                np.broadcast_to(np.sin(ang1)[:, None], (q, LANES))]), F32)
    return pl.pallas_call(
        functools.partial(_fourier_two_stage_kernel, q=q),
        grid=(b, F_GROUPS),
        in_specs=[slab, whole(csc), whole(csq), whole(csr), whole(tw1)],
        out_specs=pl.BlockSpec((1, l, LANES), lambda b, g: (b, 0, g), pipeline_mode=pl.Buffered(1)),
        out_shape=out_shape,
        scratch_shapes=[pltpu.VMEM((q * FOURIER_PITCH, LANES), F32), pltpu.VMEM((q * FOURIER_PITCH, LANES), F32)],
        compiler_params=_cparams("parallel", "parallel"),
        name="fourier_two_stage",
    )(z, csc, csq, csr, tw1)


def _gelu(x):
    return jax.nn.gelu(x, approximate=True)


def _chunk_mlp_kernel(zu_ref, zv_ref, g_ref, b_ref, ws_ref, bs_ref, o_ref, *, n_chunks):
    v = _layer_norm(_gelu(zv_ref[0])) * g_ref[...] + b_ref[...]
    vb = v.astype(BF16)
    for c in range(n_chunks):
        rows = slice(c * MLP_CHUNK, (c + 1) * MLP_CHUNK)
        for g in range(C_GROUPS):
            cols = slice(g * LANES, (g + 1) * LANES)
            sv = jnp.dot(ws_ref[g], vb[rows, cols], preferred_element_type=F32) + bs_ref[g]
            o_ref[0, rows, cols] = (_gelu(zu_ref[0, rows, cols]) * sv).astype(o_ref.dtype)


def _chunk_mlp_mixer(z, col_u, col_v, ln_g, ln_b, w_s, b_s):
    b, l, _ = z.shape
    dc = C_GROUPS * LANES
    tm = min(l, 512)
    bs_full = jnp.broadcast_to(b_s[:, :, None], (C_GROUPS, MLP_CHUNK, LANES)).astype(F32)

    def whole(a):
        return pl.BlockSpec(a.shape, lambda b, i: (0,) * a.ndim)

    ws = w_s.astype(BF16)
    g2, b2 = ln_g.reshape(1, dc), ln_b.reshape(1, dc)
    return pl.pallas_call(
        functools.partial(_chunk_mlp_kernel, n_chunks=tm // MLP_CHUNK),
        grid=(b, l // tm),
        in_specs=[pl.BlockSpec((1, tm, dc), lambda b, i: (b, i, col_u)),
                  pl.BlockSpec((1, tm, dc), lambda b, i: (b, i, col_v)),
                  whole(g2), whole(b2), whole(ws), whole(bs_full)],
        out_specs=pl.BlockSpec((1, tm, dc), lambda b, i: (b, i, 0)),
        out_shape=jax.ShapeDtypeStruct((b, l, dc), BF16),
        compiler_params=_cparams("parallel", "parallel"),
        name="chunk_mlp_mixer",
    )(z, z, g2, b2, ws, bs_full)


OUTPROJ_SUBTILE = 128


def _outproj_kernel(oa_ref, yf_ref, yc_ref, w_ref, x_ref, gate_ref, lng_ref, lnb_ref,
                    sh_ref, sc_ref, x1_ref, h_ref, *, alpha):
    tm = x_ref.shape[1]
    sub = min(tm, OUTPROJ_SUBTILE)
    for r0 in range(0, tm, sub):
        rows = slice(r0, r0 + sub)
        mixed = jnp.concatenate([oa_ref[0, rows, :], yf_ref[0, rows, :].astype(BF16), yc_ref[0, rows, :]], axis=1)
        y = jnp.dot(mixed, w_ref[...], preferred_element_type=F32)
        x1 = _layer_norm(alpha * x_ref[0, rows, :] + gate_ref[0] * y) * lng_ref[...] + lnb_ref[...]
        x1_ref[0, rows, :] = x1
        h_ref[0, rows, :] = (_layer_norm(x1) * (1.0 + sc_ref[0]) + sh_ref[0]).astype(BF16)


def _out_projection(o_a, y_f, y_c, w_out, x, gate, ln_g, ln_b, shift2, scale2, alpha):
    b, l, d = x.shape
    da, df, dc = o_a.shape[-1], y_f.shape[-1], y_c.shape[-1]
    tm = min(l, 512)
    row = lambda w: pl.BlockSpec((1, tm, w), lambda b, i: (b, i, 0))

    wspec = pl.BlockSpec((da + df + dc, d), lambda b, i: (0, 0), pipeline_mode=pl.Buffered(1))
    vec = pl.BlockSpec((1, d), lambda b, i: (0, 0))
    return pl.pallas_call(
        functools.partial(_outproj_kernel, alpha=alpha),
        grid=(b, l // tm),
        in_specs=[row(da), row(df), row(dc),
                  wspec,
                  row(d), _cond_spec(gate), vec, vec, _cond_spec(shift2), _cond_spec(scale2)],
        out_specs=[row(d), row(d)],
        out_shape=[jax.ShapeDtypeStruct((b, l, d), F32), jax.ShapeDtypeStruct((b, l, d), BF16)],
        compiler_params=_cparams("parallel", "parallel"),
        name="out_projection",
    )(o_a, y_f, y_c, w_out, x, gate, ln_g.reshape(1, d), ln_b.reshape(1, d), shift2, scale2)


FFN_SUBTILE = 256


def _ffn_up_kernel(*refs, width, vertical, tm):
    if vertical:
        h_ref, hp_ref, hn_ref, wg_ref, wu_ref, cw_ref, cb_ref, o_ref, g_scr = refs
    else:
        h_ref, wg_ref, wu_ref, cw_ref, cb_ref, o_ref, g_scr = refs
    i = pl.program_id(1)
    h = h_ref[0]
    halo = width if vertical else 0
    n_ext = tm + 2 * halo
    tf = o_ref.shape[2]
    sub = min(tf, FFN_SUBTILE)
    subtiles = [slice(c0, c0 + sub) for c0 in range(0, tf, sub)]
    for cols in subtiles:
        wg = wg_ref[:, cols]
        g_scr[halo:halo + tm, cols] = jnp.dot(h, wg, preferred_element_type=F32)
        if vertical:
            not_first = (i > 0).astype(F32)
            not_last = (i < pl.num_programs(1) - 1).astype(F32)
            g_scr[0:halo, cols] = jnp.dot(hp_ref[0], wg, preferred_element_type=F32) * not_first
            g_scr[halo + tm:, cols] = jnp.dot(hn_ref[0], wg, preferred_element_type=F32) * not_last
    colpos = lax.broadcasted_iota(jnp.int32, (n_ext, sub), 0) % width
    for cols in subtiles:
        ext = g_scr[:, cols]
        left = jnp.where(colpos == 0, 0.0, pltpu.roll(ext, 1, axis=0))
        right = jnp.where(colpos == width - 1, 0.0, pltpu.roll(ext, n_ext - 1, axis=0))
        taps = (left, ext, right)
        acc = None
        for dh in (range(3) if vertical else (1,)):
            base = halo + (dh - 1) * width
            for dw in range(3):
                term = taps[dw][base:base + tm, :] * cw_ref[dh * 3 + dw:dh * 3 + dw + 1, cols]
                acc = term if acc is None else acc + term
        gate = _silu(acc + cb_ref[:, cols])
        up = jnp.dot(h, wu_ref[:, cols], preferred_element_type=F32)
        o_ref[0, :, cols] = (gate * up).astype(o_ref.dtype)


def _ffn_up(h, w_gate, w_up, conv_w, conv_b, width):
    b, l, d = h.shape
    f = w_gate.shape[1]
    vertical = l > width
    tm = min(l, 1024)
    tf = 512
    assert tm % width == 0 and f % tf == 0
    rows_per_tile = tm // width
    n_rows = l // width
    in_specs = [pl.BlockSpec((1, tm, d), lambda b, i, j: (b, i, 0))]
    args = [h]
    if vertical:
        in_specs += [
            pl.BlockSpec((1, width, d), lambda b, i, j: (b, jnp.maximum(i * rows_per_tile - 1, 0), 0)),
            pl.BlockSpec((1, width, d), lambda b, i, j: (b, jnp.minimum((i + 1) * rows_per_tile, n_rows - 1), 0)),
        ]
        args += [h, h]
    wspec = pl.BlockSpec((d, tf), lambda b, i, j: (0, j))
    in_specs += [wspec, wspec,
                 pl.BlockSpec((9, tf), lambda b, i, j: (0, j)),
                 pl.BlockSpec((1, tf), lambda b, i, j: (0, j))]
    args += [w_gate, w_up, conv_w.reshape(9, f), conv_b.reshape(1, f)]
    ext_rows = tm + (2 * width if vertical else 0)
    return pl.pallas_call(
        functools.partial(_ffn_up_kernel, width=width, vertical=vertical, tm=tm),
        grid=(b, l // tm, f // tf),
        in_specs=in_specs,
        out_specs=pl.BlockSpec((1, tm, tf), lambda b, i, j: (b, i, j)),
        out_shape=jax.ShapeDtypeStruct((b, l, f), BF16),
        scratch_shapes=[pltpu.VMEM((ext_rows, tf), F32)],
        compiler_params=_cparams("parallel", "parallel", "arbitrary"),
        name="ffn_up",
    )(*args)


def _ffn_down_kernel(hid_ref, w_ref, x_ref, gate_ref, lng_ref, lnb_ref, o_ref, *, alpha):
    f = jnp.dot(hid_ref[0], w_ref[...], preferred_element_type=F32)
    y = alpha * x_ref[0] + gate_ref[0] * f
    o_ref[0] = _layer_norm(y) * lng_ref[...] + lnb_ref[...]


def _ffn_down(hid, w_down, x, gate, ln_g, ln_b, alpha):
    b, l, f = hid.shape
    d = x.shape[-1]
    tm = min(l, 256)
    vec = pl.BlockSpec((1, d), lambda b, i: (0, 0))
    return pl.pallas_call(
        functools.partial(_ffn_down_kernel, alpha=alpha),
        grid=(b, l // tm),
        in_specs=[pl.BlockSpec((1, tm, f), lambda b, i: (b, i, 0)),
                  pl.BlockSpec((f, d), lambda b, i: (0, 0), pipeline_mode=pl.Buffered(1)),
                  pl.BlockSpec((1, tm, d), lambda b, i: (b, i, 0)),
                  _cond_spec(gate), vec, vec],
        out_specs=pl.BlockSpec((1, tm, d), lambda b, i: (b, i, 0)),
        out_shape=jax.ShapeDtypeStruct((b, l, d), F32),
        compiler_params=_cparams("parallel", "parallel"),
        name="ffn_down",
    )(hid, w_down, x, gate, ln_g.reshape(1, d), ln_b.reshape(1, d))


def kernel(x, c, ctx, c_ctx, w_ada, b_ada, w_in, lower_bounds, a_norm_g, sg_norm_g, sg_norm_b,
           w_spatial, b_spatial, w_out, ln1_g, ln1_b, w_gate, w_up, conv_w, conv_b, w_down, ln2_g, ln2_b):
    batch, seq, d = x.shape
    depth = w_in.shape[0]
    d_a = a_norm_g.shape[-1]
    d_c = sg_norm_g.shape[-1]
    d_f = w_in.shape[-1] - 5 * d_a - 2 * d_c
    assert d_a == A_HEADS * LANES and d_c == C_GROUPS * LANES and d_f == F_GROUPS * LANES
    alpha = (2 * depth) ** 0.25
    a_blk = d_a // LANES

    lbs = jnp.cumsum(jax.nn.softmax(lower_bounds.astype(F32), axis=1), axis=1)
    lbs = lbs - lbs[:, :1]

    rows = -(-(batch + 1) // SUBLANES) * SUBLANES
    cond = jnp.zeros((rows, d), F32).at[:batch].set(c).at[batch].set(c_ctx)
    mods = _ada_modulation(cond, w_ada, b_ada)

    w_in_b, w_out_b = w_in.astype(BF16), w_out.astype(BF16)
    w_gate_b, w_up_b, w_down_b = w_gate.astype(BF16), w_up.astype(BF16), w_down.astype(BF16)
    zero_state = jnp.zeros((batch, A_HEADS, LANES, LANES), F32)

    def mixers(z, s_fwd, s_bwd, l, with_output=True):
        o_b, sf_b = _hgrn2_scan(z, 0, a_blk, 3 * a_blk, lbs[1, l], s_bwd, reverse=True)
        if not with_output:
            _, sf_f = _hgrn2_scan(z, 0, a_blk, 2 * a_blk, lbs[0, l], s_fwd, reverse=False)
            return None, sf_f, sf_b
        o_a, sf_f = _hgrn2_scan(z, 0, a_blk, 2 * a_blk, lbs[0, l], s_fwd, reverse=False,
                                final=(4 * a_blk, o_b, a_norm_g[l]))
        y_f = _fourier_mixer(z, 5 * a_blk)
        cu = (5 * d_a + d_f) // d_c
        y_c = _chunk_mlp_mixer(z, cu, cu + 1, sg_norm_g[l], sg_norm_b[l], w_spatial[l], b_spatial[l])
        return (o_a, y_f, y_c), sf_f, sf_b

    def rest_of_layer(xs, ys, m, l, width):
        sh2, sc2 = m[3], m[4]
        x1, h2 = _out_projection(*ys, w_out_b[l], xs, m[2], ln1_g[l], ln1_b[l], sh2, sc2, alpha)
        hid = _ffn_up(h2, w_gate_b[l], w_up_b[l], conv_w[l], conv_b[l], width)
        return _ffn_down(hid, w_down_b[l], x1, m[5], ln2_g[l], ln2_b[l], alpha)

    for l in range(depth):
        last = l == depth - 1
        m_all = [mods[l, :, i * d:(i + 1) * d] for i in range(6)]
        m_x = [t[:batch].reshape(batch, 1, d) for t in m_all]
        m_c = [t[batch:batch + 1].reshape(1, 1, d) for t in m_all]

        zc = _in_projection(ctx, m_c[0], m_c[1], w_in_b[l])
        ys_c, s_f, s_b = mixers(zc, zero_state, zero_state, l, with_output=not last)
        zx = _in_projection(x, m_x[0], m_x[1], w_in_b[l])
        ys_x, _, _ = mixers(zx, s_f, s_b, l)
        x = rest_of_layer(x, ys_x, m_x, l, GRID_W)
        if not last:
            ctx = rest_of_layer(ctx, ys_c, m_c, l, ctx.shape[1])
    return x
```

```python
import functools
import math

import numpy as np
import jax
import jax.numpy as jnp
from jax import lax
from jax.experimental import pallas as pl
from jax.experimental.pallas import tpu as pltpu

F32 = jnp.float32
BF16 = jnp.bfloat16

LANES = 128
SUBLANES = 8
VMEM_LIMIT_BYTES = 56 * 1024 * 1024

A_HEADS = 8
F_GROUPS = 4
C_GROUPS = 4
MLP_CHUNK = 128
SCAN_BLOCK = 128
GRID_W = 64
LN_EPS = 1e-6
RMS_EPS = 1e-6


def _cparams(*sem):
    return pltpu.CompilerParams(dimension_semantics=sem, vmem_limit_bytes=VMEM_LIMIT_BYTES)


def _silu(x):
    return x * jax.nn.sigmoid(x)


def _silu_tanh(x):
    hx = 0.5 * x
    return hx + hx * jnp.tanh(hx)


def _layer_norm(x):
    mu = jnp.mean(x, axis=-1, keepdims=True)
    xc = x - mu
    var = jnp.mean(xc * xc, axis=-1, keepdims=True)
    return xc * lax.rsqrt(var + LN_EPS)


def _pick_tile(n, cap, quantum=LANES):
    return max(t for t in range(quantum, cap + 1, quantum) if n % t == 0)


def _cond_spec(arr):
    d = arr.shape[-1]
    if arr.shape[0] == 1:
        return pl.BlockSpec((1, 1, d), lambda b, *_: (0, 0, 0))
    return pl.BlockSpec((1, 1, d), lambda b, *_: (b, 0, 0))


def _ada_kernel(cond_ref, w_ref, b_ref, o_ref):
    s = _silu(cond_ref[...]).astype(BF16)
    o_ref[0] = jnp.dot(s, w_ref[0].astype(BF16), preferred_element_type=F32) + b_ref[0]


def _ada_modulation(cond, w_ada, b_ada):
    depth, d, n = w_ada.shape
    rows = cond.shape[0]
    tn = 1024
    return pl.pallas_call(
        _ada_kernel,
        grid=(depth, n // tn),
        in_specs=[
            pl.BlockSpec((rows, d), lambda l, j: (0, 0)),
            pl.BlockSpec((1, d, tn), lambda l, j: (l, 0, j)),
            pl.BlockSpec((1, 1, tn), lambda l, j: (l, 0, j)),
        ],
        out_specs=pl.BlockSpec((1, rows, tn), lambda l, j: (l, 0, j)),
        out_shape=jax.ShapeDtypeStruct((depth, rows, n), F32),
        compiler_params=_cparams("parallel", "parallel"),
        name="ada_modulation",
    )(cond, w_ada, b_ada.reshape(depth, 1, n))


def _inproj_kernel(x_ref, sh_ref, sc_ref, w_ref, o_ref, h_scr):
    @pl.when(pl.program_id(2) == 0)
    def _():
        h = _layer_norm(x_ref[0]) * (1.0 + sc_ref[0]) + sh_ref[0]
        h_scr[...] = h.astype(BF16)

    o_ref[0] = jnp.dot(h_scr[...], w_ref[...], preferred_element_type=F32)


def _in_projection(x, shift, scale, w, layer):
    b, l, d = x.shape
    n = w.shape[2]
    tm = min(l, 1024)
    tn = _pick_tile(n, 1792)
    return pl.pallas_call(
        _inproj_kernel,
        grid=(b, l // tm, n // tn),
        in_specs=[
            pl.BlockSpec((1, tm, d), lambda b, i, j: (b, i, 0)),
            _cond_spec(shift),
            _cond_spec(scale),
            pl.BlockSpec((None, d, tn), lambda b, i, j: (layer, 0, j)),
        ],
        out_specs=pl.BlockSpec((1, tm, tn), lambda b, i, j: (b, i, j)),
        out_shape=jax.ShapeDtypeStruct((b, l, n), F32),
        scratch_shapes=[pltpu.VMEM((tm, d), BF16)],
        compiler_params=_cparams("parallel", "parallel", "arbitrary"),
        name="in_projection",
    )(x, shift, scale, w)


LOG2E = 1.4426950408889634
SCAN_HEADS_PER_STEP = 4
SCAN_STAGE_LAGS = (3, 2)
_HIGH_LEVELS = (8, 16, 32, 64)
_M_DIAG, _M_G1, _M_G2, _M_G4, _M_Q1, _M_K1, _M_Q2, _M_Q4, _M_H8, _M_H16, _M_H32 = range(11)
_R_Q1, _R_K1, _R_Q2, _R_Q4 = range(4)


def _scan_constants(reverse):
    nb = SCAN_BLOCK
    row = np.arange(nb)[:, None]
    col = np.arange(nb)[None, :]
    x = row ^ col
    tabs = np.zeros((11, nb, nb), np.float32)
    tabs[_M_DIAG] = x == 0
    tabs[_M_G1], tabs[_M_G2], tabs[_M_G4] = x < 2, x < 4, x < 8

    def q_role(m):
        upper = (row & m) == 0
        return np.broadcast_to(upper if reverse else ~upper, (nb, nb))

    tabs[_M_Q1], tabs[_M_Q2], tabs[_M_Q4] = q_role(1), q_role(2), q_role(4)
    tabs[_M_K1] = 1.0 - tabs[_M_Q1]
    for idx, m in ((_M_H8, 8), (_M_H16, 16), (_M_H32, 32)):
        grp = np.arange(nb // 2)[:, None] // m
        tabs[idx, :nb // 2] = (col // (2 * m)) == grp
    tri = (col >= row) if reverse else (col <= row)
    roles = jnp.asarray(tabs[[_M_Q1, _M_K1, _M_Q2, _M_Q4]], BF16)
    return jnp.asarray(tabs), roles, jnp.asarray(np.concatenate([tri, tri], axis=1), BF16)


def _dot_nt(a, b):
    return lax.dot_general(a, b, (((1,), (1,)), ((), ())), preferred_element_type=F32)


def _dot_tn(a, b):
    return lax.dot_general(a, b, (((0,), (0,)), ((), ())), preferred_element_type=F32)


def _scan_gates(zq, zf, lb, c_slot, tri2):
    q = _silu_tanh(zq)
    l2_lb = jnp.log2(lb)
    e = jnp.exp2(-LOG2E * jnp.abs(zf))
    a2 = jnp.minimum(zf, 0.0) * LOG2E - jnp.log2(1.0 + e) + jnp.log1p(-lb) * LOG2E
    g2 = jnp.maximum(a2, l2_lb) + jnp.log2(1.0 + jnp.exp2(-jnp.abs(a2 - l2_lb)))
    f = jnp.exp2(g2)
    k = 1.0 - f
    g_hi = g2.astype(BF16)
    g_lo = (g2 - g_hi.astype(F32)).astype(BF16)
    c2 = jnp.dot(tri2, jnp.concatenate([g_hi, g_lo], axis=0), preferred_element_type=F32)
    c_slot[...] = c2
    return q, k, f, c2


def _scan_scores(q, k, f, c2, c_slot, tab_ref, role_ref, *, reverse):
    nb = SCAN_BLOCK
    half = nb // 2
    n_tiles = nb // SUBLANES
    tile = lambda x, i: x[i * SUBLANES:(i + 1) * SUBLANES]
    qb, kb = q.astype(BF16), k.astype(BF16)

    def crow(r):
        return c_slot[r:r + 1, :]

    lvl = _dot_nt(qb, kb) * tab_ref[_M_DIAG]
    qt = qb * (f.astype(BF16) * role_ref[_R_Q1])
    kt = kb * role_ref[_R_K1]
    lvl = lvl + _dot_nt(qt, kt) * tab_ref[_M_G1]

    upper_half_tile = lax.broadcasted_iota(jnp.int32, (SUBLANES, LANES), 0) >= SUBLANES // 2
    for m, rq, mg in ((2, _R_Q2, _M_G2), (4, _R_Q4, _M_G4)):
        refs = []
        for t in range(n_tiles):
            r0 = t * SUBLANES + (m if reverse else m - 1)
            rb = jnp.broadcast_to(crow(r0), (SUBLANES, LANES))
            if m == 2:
                rb = jnp.where(upper_half_tile, jnp.broadcast_to(crow(r0 + 4), (SUBLANES, LANES)), rb)
            refs.append(rb)
        w = jnp.exp2(-jnp.abs(c2 - jnp.concatenate(refs, axis=0))).astype(BF16)
        wq = w * role_ref[rq]
        wk = w - wq
        lvl = lvl + _dot_nt(qb * wq, kb * wk) * tab_ref[mg]
    score_tiles = [tile(lvl, i) for i in range(n_tiles)]

    for m, mh in zip(_HIGH_LEVELS, (_M_H8, _M_H16, _M_H32, None)):
        packed = m % (2 * SUBLANES) == 0
        q_parts, k_parts, q_tiles = [], [], []
        for g in range(nb // (2 * m)):
            base = g * 2 * m
            first, second = slice(base, base + m), slice(base + m, base + 2 * m)
            q_rows, k_rows = (first, second) if reverse else (second, first)
            c_ref = crow(base + m if reverse else base + m - 1)
            wq = jnp.exp2(c2[q_rows] - c_ref)
            wk = jnp.exp2(c_ref - c2[k_rows])
            if packed:
                q_parts.append(qb[q_rows] * wq.astype(BF16))
                k_part = kb[k_rows] * wk.astype(BF16)
                zeros = jnp.zeros((m, LANES), BF16)
            else:
                q_parts.append(q[q_rows] * wq)
                k_part = k[k_rows] * wk
                zeros = jnp.zeros((m, LANES), F32)
            k_parts += [zeros, k_part] if reverse else [k_part, zeros]
            q_tiles += list(range(q_rows.start // SUBLANES, q_rows.stop // SUBLANES))
        res = _dot_nt(jnp.concatenate(q_parts, axis=0).astype(BF16),
                      jnp.concatenate(k_parts, axis=0).astype(BF16))
        if mh is not None:
            res = res * tab_ref[mh, 0:half, :]
        for j, t in enumerate(q_tiles):
            score_tiles[t] = score_tiles[t] + tile(res, j)
    return jnp.concatenate(score_tiles, axis=0), qb, kb


def _scan_kernel(*refs, reverse, finalize, n_sub):
    if finalize:
        (zq_ref, zv_ref, zf_ref, lb_ref, s0_ref, tab_ref, role_ref, tri_ref, zg_ref, oo_ref, ng_ref,
         o_ref, sfin_ref, s_scr, c_scr) = refs
    else:
        (zq_ref, zv_ref, zf_ref, lb_ref, s0_ref, tab_ref, role_ref, tri_ref,
         o_ref, sfin_ref, s_scr, c_scr) = refs
    nb = SCAN_BLOCK
    hps = SCAN_HEADS_PER_STEP
    ci = pl.program_id(2)

    @pl.when(ci == 0)
    def _():
        s_scr[...] = s0_ref[0]

    tri2 = tri_ref[...]
    states = [s_scr[hh] for hh in range(hps)]
    order = [(n_sub - 1 - i) if reverse else i for i in range(n_sub)]
    items = [(j, hh) for j in order for hh in range(hps)]
    where = {it: (slice(it[0] * nb, (it[0] + 1) * nb), slice(it[1] * LANES, (it[1] + 1) * LANES)) for it in items}
    slot = {it: c_scr.at[it[0] * hps + it[1]] for it in items}
    act, scores, packed = {}, {}, {}

    def stage_gates(it):
        rows, cols = where[it]
        act[it] = _scan_gates(zq_ref[0, rows, cols], zf_ref[0, rows, cols], lb_ref[0, :, cols], slot[it], tri2)

    def stage_scores(it):
        q, k, f, c2 = act[it]
        scores[it], *packed[it] = _scan_scores(q, k, f, c2, slot[it], tab_ref, role_ref, reverse=reverse)

    def stage_output(it):
        rows, cols = where[it]
        hh = it[1]
        _, _, _, c2 = act.pop(it)
        qb, kb = packed.pop(it)
        v = zv_ref[0, rows, cols].astype(BF16)
        c_tot = slot[it][0:1, :] if reverse else slot[it][nb - 1:nb, :]
        o = jnp.dot(scores.pop(it).astype(BF16), v, preferred_element_type=F32)
        o = o + _dot_nt(qb * jnp.exp2(c2).astype(BF16), states[hh].astype(BF16))
        states[hh] = states[hh] * jnp.exp2(c_tot) + _dot_tn(v, kb * jnp.exp2(c_tot - c2).astype(BF16))
        if finalize:
            o = o + oo_ref[0, rows, cols]
            ms = jnp.mean(o * o, axis=-1, keepdims=True)
            o = o * lax.rsqrt(ms + RMS_EPS) * ng_ref[0, :, cols] * _silu_tanh(zg_ref[0, rows, cols])
        o_ref[0, rows, cols] = o.astype(o_ref.dtype)

    lag_s, lag_o = SCAN_STAGE_LAGS
    for n in range(len(items) + lag_s + lag_o):
        if n < len(items):
            stage_gates(items[n])
        if 0 <= n - lag_s < len(items):
            stage_scores(items[n - lag_s])
        if 0 <= n - lag_s - lag_o < len(items):
            stage_output(items[n - lag_s - lag_o])
    for hh in range(hps):
        s_scr[hh] = states[hh]

    @pl.when(ci == pl.num_programs(2) - 1)
    def _():
        sfin_ref[0] = s_scr[...]


def _hgrn2_scan(z, col_q, col_v, col_f, lb, s0, *, reverse, final=None):
    b, l, _ = z.shape
    hps = SCAN_HEADS_PER_STEP
    heads, hd, wd = A_HEADS, LANES, SCAN_HEADS_PER_STEP * LANES
    cb = min(l, 512)
    nc = l // cb
    n_sub = cb // SCAN_BLOCK
    assert heads % hps == 0 and col_q % hps == 0 and col_v % hps == 0 and col_f % hps == 0

    def chunk(i):
        return (nc - 1 - i) if reverse else i

    def zspec(col0):
        return pl.BlockSpec((1, cb, wd), lambda b, h, i: (b, chunk(i), col0 // hps + h))

    def whole(a):
        return pl.BlockSpec(a.shape, lambda b, h, i: (0,) * a.ndim)

    head_row = pl.BlockSpec((1, 1, wd), lambda b, h, i: (h, 0, 0))
    state = pl.BlockSpec((1, hps, hd, hd), lambda b, h, i: (b, h, 0, 0))
    tabs, roles, tri = _scan_constants(reverse)
    in_specs = [zspec(col_q), zspec(col_v), zspec(col_f), head_row, state, whole(tabs), whole(roles), whole(tri)]
    args = [z, z, z, lb.reshape(heads // hps, 1, wd), s0, tabs, roles, tri]
    if final is not None:
        col_g, other, norm_g = final
        assert col_g % hps == 0
        in_specs += [zspec(col_g), zspec(0), head_row]
        args += [z, other, norm_g.reshape(heads // hps, 1, wd)]
    out_dtype = BF16 if final is not None else F32
    return pl.pallas_call(
        functools.partial(_scan_kernel, reverse=reverse, finalize=final is not None, n_sub=n_sub),
        grid=(b, heads // hps, nc),
        in_specs=in_specs,
        out_specs=[zspec(0), state],
        out_shape=[jax.ShapeDtypeStruct((b, l, heads * hd), out_dtype),
                   jax.ShapeDtypeStruct((b, heads, hd, hd), F32)],
        scratch_shapes=[pltpu.VMEM((hps, hd, hd), F32), pltpu.VMEM((n_sub * hps, SCAN_BLOCK, hd), F32)],
        compiler_params=_cparams("parallel", "parallel", "arbitrary"),
        name="hgrn2_scan_bwd" if reverse else "hgrn2_scan_fwd",
    )(*args)


def _dft_cos_sin(n):
    idx = np.arange(n, dtype=np.int64)
    ang = 2.0 * np.pi * ((idx[:, None] * idx[None, :]) % n).astype(np.float64) / n
    return np.cos(ang), np.sin(ang)


FOURIER_PITCH = LANES + SUBLANES


def _fourier_two_stage_kernel(z_ref, csc_ref, csq_ref, csr_ref, tw1_ref, o_ref, tre_scr, tim_scr, *, q):
    nr = LANES
    pitch = FOURIER_PITCH
    csc = csc_ref[...]
    csq = csq_ref[...]
    tw_c1 = tw1_ref[0]
    tw_s1 = tw1_ref[1]

    def stage_a(r, tw):
        tw_c, tw_s = tw
        x = z_ref[0, pl.ds(r, q, stride=nr), :]
        uv = jnp.dot(x.astype(BF16), csc, preferred_element_type=F32)
        res = jnp.dot(csq, uv.astype(BF16), preferred_element_type=F32)
        cu, cv = res[:q, :LANES], res[:q, LANES:]
        su, sv = res[q:, :LANES], res[q:, LANES:]
        g_re = cu - sv
        g_im = -(cv + su)
        tre_scr[pl.ds(r, q, stride=pitch), :] = g_re * tw_c + g_im * tw_s
        tim_scr[pl.ds(r, q, stride=pitch), :] = g_im * tw_c - g_re * tw_s
        return tw_c * tw_c1 - tw_s * tw_s1, tw_s * tw_c1 + tw_c * tw_s1

    lax.fori_loop(0, nr, stage_a, (jnp.ones((q, LANES), F32), jnp.zeros((q, LANES), F32)), unroll=4)

    csr = csr_ref[...]

    def stage_b(ka, carry):
        off = pl.multiple_of(ka * pitch, SUBLANES)
        t = jnp.concatenate([tre_scr[pl.ds(off, nr), :], tim_scr[pl.ds(off, nr), :]], axis=0)
        y = jnp.dot(csr, t.astype(BF16), preferred_element_type=F32)
        o_ref[0, pl.ds(ka, nr, stride=q), :] = y
        return carry

    lax.fori_loop(0, q, stage_b, 0, unroll=4)


def _fourier_direct_kernel(z_ref, csc_ref, csl_ref, o_ref):
    uv = jnp.dot(z_ref[0].astype(BF16), csc_ref[...], preferred_element_type=F32)
    t = jnp.concatenate([uv[:, :LANES], uv[:, LANES:]], axis=0)
    o_ref[0] = jnp.dot(csl_ref[...], t.astype(BF16), preferred_element_type=F32)


def _fourier_mixer(z, col0):
    b, l, _ = z.shape
    scale = 1.0 / math.sqrt(l * LANES)
    cc, sc = _dft_cos_sin(LANES)
    csc = jnp.asarray(np.concatenate([cc, sc], axis=1) * scale, BF16)
    out_shape = jax.ShapeDtypeStruct((b, l, F_GROUPS * LANES), F32)
    slab = pl.BlockSpec((1, l, LANES), lambda b, g: (b, 0, col0 + g))
    out_slab = pl.BlockSpec((1, l, LANES), lambda b, g: (b, 0, g))

    def whole(a):
        return pl.BlockSpec(a.shape, lambda b, g: (0,) * a.ndim)

    if l <= 1024:
        cl, sl = _dft_cos_sin(l)
        csl = jnp.asarray(np.concatenate([cl, -sl], axis=1), BF16)
        return pl.pallas_call(
            _fourier_direct_kernel,
            grid=(b, F_GROUPS),
            in_specs=[slab, whole(csc), whole(csl)],
            out_specs=out_slab,
            out_shape=out_shape,
            compiler_params=_cparams("parallel", "parallel"),
            name="fourier_direct",
        )(z, csc, csl)

    q = l // LANES
    cq, sq = _dft_cos_sin(q)
    csq = jnp.asarray(np.concatenate([cq, sq], axis=0), BF16)
    csr = jnp.asarray(np.concatenate([cc, sc], axis=1), BF16)
    ang1 = 2.0 * np.pi * np.arange(q, dtype=np.float64) / l
    tw1 = jnp.asarray(np.stack([np.broadcast_to(np.cos(ang1)[:, None], (q, LANES)),
                                np.broadcast_to(np.sin(ang1)[:, None], (q, LANES))]), F32)
    return pl.pallas_call(
        functools.partial(_fourier_two_stage_kernel, q=q),
        grid=(b, F_GROUPS),
        in_specs=[slab, whole(csc), whole(csq), whole(csr), whole(tw1)],
        out_specs=pl.BlockSpec((1, l, LANES), lambda b, g: (b, 0, g), pipeline_mode=pl.Buffered(1)),
        out_shape=out_shape,
        scratch_shapes=[pltpu.VMEM((q * FOURIER_PITCH, LANES), F32), pltpu.VMEM((q * FOURIER_PITCH, LANES), F32)],
        compiler_params=_cparams("parallel", "parallel"),
        name="fourier_two_stage",
    )(z, csc, csq, csr, tw1)


def _gelu(x):
    return jax.nn.gelu(x, approximate=True)


def _chunk_mlp_kernel(zu_ref, zv_ref, g_ref, b_ref, ws_ref, bs_ref, o_ref, *, n_chunks):
    v = _layer_norm(_gelu(zv_ref[0])) * g_ref[...] + b_ref[...]
    vb = v.astype(BF16)
    for c in range(n_chunks):
        rows = slice(c * MLP_CHUNK, (c + 1) * MLP_CHUNK)
        for g in range(C_GROUPS):
            cols = slice(g * LANES, (g + 1) * LANES)
            sv = jnp.dot(ws_ref[g], vb[rows, cols], preferred_element_type=F32) + bs_ref[g]
            o_ref[0, rows, cols] = (_gelu(zu_ref[0, rows, cols]) * sv).astype(o_ref.dtype)


def _chunk_mlp_mixer(z, col_u, col_v, ln_g, ln_b, w_s, b_s):
    b, l, _ = z.shape
    dc = C_GROUPS * LANES
    tm = min(l, 512)
    bs_full = jnp.broadcast_to(b_s[:, :, None], (C_GROUPS, MLP_CHUNK, LANES)).astype(F32)

    def whole(a):
        return pl.BlockSpec(a.shape, lambda b, i: (0,) * a.ndim)

    ws = w_s.astype(BF16)
    g2, b2 = ln_g.reshape(1, dc), ln_b.reshape(1, dc)
    return pl.pallas_call(
        functools.partial(_chunk_mlp_kernel, n_chunks=tm // MLP_CHUNK),
        grid=(b, l // tm),
        in_specs=[pl.BlockSpec((1, tm, dc), lambda b, i: (b, i, col_u)),
                  pl.BlockSpec((1, tm, dc), lambda b, i: (b, i, col_v)),
                  whole(g2), whole(b2), whole(ws), whole(bs_full)],
        out_specs=pl.BlockSpec((1, tm, dc), lambda b, i: (b, i, 0)),
        out_shape=jax.ShapeDtypeStruct((b, l, dc), BF16),
        compiler_params=_cparams("parallel", "parallel"),
        name="chunk_mlp_mixer",
    )(z, z, g2, b2, ws, bs_full)


OUTPROJ_SUBTILE = 128


def _outproj_kernel(oa_ref, yf_ref, yc_ref, w_ref, x_ref, gate_ref, lng_ref, lnb_ref,
                    sh_ref, sc_ref, x1_ref, h_ref, *, alpha):
    tm = x_ref.shape[1]
    sub = min(tm, OUTPROJ_SUBTILE)
    for r0 in range(0, tm, sub):
        rows = slice(r0, r0 + sub)
        mixed = jnp.concatenate([oa_ref[0, rows, :], yf_ref[0, rows, :].astype(BF16), yc_ref[0, rows, :]], axis=1)
        y = jnp.dot(mixed, w_ref[...], preferred_element_type=F32)
        x1 = _layer_norm(alpha * x_ref[0, rows, :] + gate_ref[0] * y) * lng_ref[...] + lnb_ref[...]
        x1_ref[0, rows, :] = x1
        h_ref[0, rows, :] = (_layer_norm(x1) * (1.0 + sc_ref[0]) + sh_ref[0]).astype(BF16)


def _out_projection(o_a, y_f, y_c, w_out, layer, x, gate, ln_g, ln_b, shift2, scale2, alpha):
    b, l, d = x.shape
    da, df, dc = o_a.shape[-1], y_f.shape[-1], y_c.shape[-1]
    tm = min(l, 512)
    row = lambda w: pl.BlockSpec((1, tm, w), lambda b, i: (b, i, 0))

    wspec = pl.BlockSpec((None, da + df + dc, d), lambda b, i: (layer, 0, 0), pipeline_mode=pl.Buffered(1))
    vec = pl.BlockSpec((1, d), lambda b, i: (0, 0))
    return pl.pallas_call(
        functools.partial(_outproj_kernel, alpha=alpha),
        grid=(b, l // tm),
        in_specs=[row(da), row(df), row(dc),
                  wspec,
                  row(d), _cond_spec(gate), vec, vec, _cond_spec(shift2), _cond_spec(scale2)],
        out_specs=[row(d), row(d)],
        out_shape=[jax.ShapeDtypeStruct((b, l, d), F32), jax.ShapeDtypeStruct((b, l, d), BF16)],
        compiler_params=_cparams("parallel", "parallel"),
        name="out_projection",
    )(o_a, y_f, y_c, w_out, x, gate, ln_g.reshape(1, d), ln_b.reshape(1, d), shift2, scale2)


FFN_SUBTILE = 256


def _ffn_up_kernel(*refs, width, vertical, tm):
    if vertical:
        h_ref, hp_ref, hn_ref, wg_ref, wu_ref, cw_ref, cb_ref, o_ref, g_scr = refs
    else:
        h_ref, wg_ref, wu_ref, cw_ref, cb_ref, o_ref, g_scr = refs
    i = pl.program_id(1)
    h = h_ref[0]
    halo = width if vertical else 0
    n_ext = tm + 2 * halo
    tf = o_ref.shape[2]
    sub = min(tf, FFN_SUBTILE)
    subtiles = [slice(c0, c0 + sub) for c0 in range(0, tf, sub)]
    for cols in subtiles:
        wg = wg_ref[:, cols]
        g_scr[halo:halo + tm, cols] = jnp.dot(h, wg, preferred_element_type=F32)
        if vertical:
            not_first = (i > 0).astype(F32)
            not_last = (i < pl.num_programs(1) - 1).astype(F32)
            g_scr[0:halo, cols] = jnp.dot(hp_ref[0], wg, preferred_element_type=F32) * not_first
            g_scr[halo + tm:, cols] = jnp.dot(hn_ref[0], wg, preferred_element_type=F32) * not_last
    colpos = lax.broadcasted_iota(jnp.int32, (n_ext, sub), 0) % width
    for cols in subtiles:
        ext = g_scr[:, cols]
        left = jnp.where(colpos == 0, 0.0, pltpu.roll(ext, 1, axis=0))
        right = jnp.where(colpos == width - 1, 0.0, pltpu.roll(ext, n_ext - 1, axis=0))
        taps = (left, ext, right)
        acc = None
        for dh in (range(3) if vertical else (1,)):
            base = halo + (dh - 1) * width
            for dw in range(3):
                term = taps[dw][base:base + tm, :] * cw_ref[dh * 3 + dw:dh * 3 + dw + 1, cols]
                acc = term if acc is None else acc + term
        gate = _silu(acc + cb_ref[:, cols])
        up = jnp.dot(h, wu_ref[:, cols], preferred_element_type=F32)
        o_ref[0, :, cols] = (gate * up).astype(o_ref.dtype)


def _ffn_up(h, w_gate, w_up, conv_w, conv_b, layer, width):
    b, l, d = h.shape
    depth, _, f = w_gate.shape
    vertical = l > width
    tm = min(l, 1024)
    tf = 512
    assert tm % width == 0 and f % tf == 0
    rows_per_tile = tm // width
    n_rows = l // width
    in_specs = [pl.BlockSpec((1, tm, d), lambda b, i, j: (b, i, 0))]
    args = [h]
    if vertical:
        in_specs += [
            pl.BlockSpec((1, width, d), lambda b, i, j: (b, jnp.maximum(i * rows_per_tile - 1, 0), 0)),
            pl.BlockSpec((1, width, d), lambda b, i, j: (b, jnp.minimum((i + 1) * rows_per_tile, n_rows - 1), 0)),
        ]
        args += [h, h]
    wspec = pl.BlockSpec((None, d, tf), lambda b, i, j: (layer, 0, j))
    in_specs += [wspec, wspec,
                 pl.BlockSpec((None, 9, tf), lambda b, i, j: (layer, 0, j)),
                 pl.BlockSpec((None, 1, tf), lambda b, i, j: (layer, 0, j))]
    args += [w_gate, w_up, conv_w.reshape(depth, 9, f), conv_b.reshape(depth, 1, f)]
    ext_rows = tm + (2 * width if vertical else 0)
    return pl.pallas_call(
        functools.partial(_ffn_up_kernel, width=width, vertical=vertical, tm=tm),
        grid=(b, l // tm, f // tf),
        in_specs=in_specs,
        out_specs=pl.BlockSpec((1, tm, tf), lambda b, i, j: (b, i, j)),
        out_shape=jax.ShapeDtypeStruct((b, l, f), BF16),
        scratch_shapes=[pltpu.VMEM((ext_rows, tf), F32)],
        compiler_params=_cparams("parallel", "parallel", "arbitrary"),
        name="ffn_up",
    )(*args)


def _ffn_down_kernel(hid_ref, w_ref, x_ref, gate_ref, lng_ref, lnb_ref, o_ref, *, alpha):
    f = jnp.dot(hid_ref[0], w_ref[...], preferred_element_type=F32)
    y = alpha * x_ref[0] + gate_ref[0] * f
    o_ref[0] = _layer_norm(y) * lng_ref[...] + lnb_ref[...]


def _ffn_down(hid, w_down, layer, x, gate, ln_g, ln_b, alpha):
    b, l, f = hid.shape
    d = x.shape[-1]
    tm = min(l, 256)
    vec = pl.BlockSpec((1, d), lambda b, i: (0, 0))
    return pl.pallas_call(
        functools.partial(_ffn_down_kernel, alpha=alpha),
        grid=(b, l // tm),
        in_specs=[pl.BlockSpec((1, tm, f), lambda b, i: (b, i, 0)),
                  pl.BlockSpec((None, f, d), lambda b, i: (layer, 0, 0), pipeline_mode=pl.Buffered(1)),
                  pl.BlockSpec((1, tm, d), lambda b, i: (b, i, 0)),
                  _cond_spec(gate), vec, vec],
        out_specs=pl.BlockSpec((1, tm, d), lambda b, i: (b, i, 0)),
        out_shape=jax.ShapeDtypeStruct((b, l, d), F32),
        compiler_params=_cparams("parallel", "parallel"),
        name="ffn_down",
    )(hid, w_down, x, gate, ln_g.reshape(1, d), ln_b.reshape(1, d))


def kernel(x, c, ctx, c_ctx, w_ada, b_ada, w_in, lower_bounds, a_norm_g, sg_norm_g, sg_norm_b,
           w_spatial, b_spatial, w_out, ln1_g, ln1_b, w_gate, w_up, conv_w, conv_b, w_down, ln2_g, ln2_b):
    batch, seq, d = x.shape
    depth = w_in.shape[0]
    d_a = a_norm_g.shape[-1]
    d_c = sg_norm_g.shape[-1]
    d_f = w_in.shape[-1] - 5 * d_a - 2 * d_c
    assert d_a == A_HEADS * LANES and d_c == C_GROUPS * LANES and d_f == F_GROUPS * LANES
    alpha = (2 * depth) ** 0.25
    a_blk = d_a // LANES

    lbs = jnp.cumsum(jax.nn.softmax(lower_bounds.astype(F32), axis=1), axis=1)
    lbs = lbs - lbs[:, :1]

    rows = -(-(batch + 1) // SUBLANES) * SUBLANES
    cond = jnp.zeros((rows, d), F32).at[:batch].set(c).at[batch].set(c_ctx)
    mods = _ada_modulation(cond, w_ada, b_ada)

    w_in_b, w_out_b = w_in.astype(BF16), w_out.astype(BF16)
    w_gate_b, w_up_b, w_down_b = w_gate.astype(BF16), w_up.astype(BF16), w_down.astype(BF16)
    zero_state = jnp.zeros((batch, A_HEADS, LANES, LANES), F32)

    def mixers(z, s_fwd, s_bwd, l, with_output=True):
        o_b, sf_b = _hgrn2_scan(z, 0, a_blk, 3 * a_blk, lbs[1, l], s_bwd, reverse=True)
        if not with_output:
            _, sf_f = _hgrn2_scan(z, 0, a_blk, 2 * a_blk, lbs[0, l], s_fwd, reverse=False)
            return None, sf_f, sf_b
        o_a, sf_f = _hgrn2_scan(z, 0, a_blk, 2 * a_blk, lbs[0, l], s_fwd, reverse=False,
                                final=(4 * a_blk, o_b, a_norm_g[l]))
        y_f = _fourier_mixer(z, 5 * a_blk)
        cu = (5 * d_a + d_f) // d_c
        y_c = _chunk_mlp_mixer(z, cu, cu + 1, sg_norm_g[l], sg_norm_b[l], w_spatial[l], b_spatial[l])
        return (o_a, y_f, y_c), sf_f, sf_b

    def rest_of_layer(xs, ys, m, l, width):
        sh2, sc2 = m[3], m[4]
        x1, h2 = _out_projection(*ys, w_out_b, l, xs, m[2], ln1_g[l], ln1_b[l], sh2, sc2, alpha)
        hid = _ffn_up(h2, w_gate_b, w_up_b, conv_w, conv_b, l, width)
        return _ffn_down(hid, w_down_b, l, x1, m[5], ln2_g[l], ln2_b[l], alpha)

    for l in range(depth):
        last = l == depth - 1
        m_all = [mods[l, :, i * d:(i + 1) * d] for i in range(6)]
        m_x = [t[:batch].reshape(batch, 1, d) for t in m_all]
        m_c = [t[batch:batch + 1].reshape(1, 1, d) for t in m_all]

        zc = _in_projection(ctx, m_c[0], m_c[1], w_in_b, l)
        ys_c, s_f, s_b = mixers(zc, zero_state, zero_state, l, with_output=not last)
        zx = _in_projection(x, m_x[0], m_x[1], w_in_b, l)
        ys_x, _, _ = mixers(zx, s_f, s_b, l)
        x = rest_of_layer(x, ys_x, m_x, l, GRID_W)
        if not last:
            ctx = rest_of_layer(ctx, ys_c, m_c, l, ctx.shape[1])
    return x
```

```python
import functools
import math

import numpy as np
import jax
import jax.numpy as jnp
from jax import lax
from jax.experimental import pallas as pl
from jax.experimental.pallas import tpu as pltpu

F32 = jnp.float32
BF16 = jnp.bfloat16

LANES = 128
SUBLANES = 8
VMEM_LIMIT_BYTES = 56 * 1024 * 1024

A_HEADS = 8
F_GROUPS = 4
C_GROUPS = 4
MLP_CHUNK = 128
SCAN_BLOCK = 128
GRID_W = 64
LN_EPS = 1e-6
RMS_EPS = 1e-6


def _cparams(*sem):
    return pltpu.CompilerParams(dimension_semantics=sem, vmem_limit_bytes=VMEM_LIMIT_BYTES)


def _silu(x):
    return x * jax.nn.sigmoid(x)


def _silu_tanh(x):
    hx = 0.5 * x
    return hx + hx * jnp.tanh(hx)


def _layer_norm(x):
    mu = jnp.mean(x, axis=-1, keepdims=True)
    xc = x - mu
    var = jnp.mean(xc * xc, axis=-1, keepdims=True)
    return xc * lax.rsqrt(var + LN_EPS)


def _pick_tile(n, cap, quantum=LANES):
    return max(t for t in range(quantum, cap + 1, quantum) if n % t == 0)


def _cond_spec(arr):
    d = arr.shape[-1]
    if arr.shape[0] == 1:
        return pl.BlockSpec((1, 1, d), lambda b, *_: (0, 0, 0))
    return pl.BlockSpec((1, 1, d), lambda b, *_: (b, 0, 0))


def _ada_kernel(cond_ref, w_ref, b_ref, o_ref):
    s = _silu(cond_ref[...]).astype(BF16)
    o_ref[0] = jnp.dot(s, w_ref[0].astype(BF16), preferred_element_type=F32) + b_ref[0]


def _ada_modulation(cond, w_ada, b_ada):
    depth, d, n = w_ada.shape
    rows = cond.shape[0]
    tn = 1024
    return pl.pallas_call(
        _ada_kernel,
        grid=(depth, n // tn),
        in_specs=[
            pl.BlockSpec((rows, d), lambda l, j: (0, 0)),
            pl.BlockSpec((1, d, tn), lambda l, j: (l, 0, j)),
            pl.BlockSpec((1, 1, tn), lambda l, j: (l, 0, j)),
        ],
        out_specs=pl.BlockSpec((1, rows, tn), lambda l, j: (l, 0, j)),
        out_shape=jax.ShapeDtypeStruct((depth, rows, n), F32),
        compiler_params=_cparams("parallel", "parallel"),
        name="ada_modulation",
    )(cond, w_ada, b_ada.reshape(depth, 1, n))


def _inproj_kernel(x_ref, sh_ref, sc_ref, w_ref, o_ref, h_scr):
    @pl.when(pl.program_id(2) == 0)
    def _():
        h = _layer_norm(x_ref[0]) * (1.0 + sc_ref[0]) + sh_ref[0]
        h_scr[...] = h.astype(BF16)

    o_ref[0] = jnp.dot(h_scr[...], w_ref[...], preferred_element_type=F32)


def _in_projection(x, shift, scale, w, layer):
    b, l, d = x.shape
    n = w.shape[2]
    tm = min(l, 1024)
    tn = _pick_tile(n, 1792)
    return pl.pallas_call(
        _inproj_kernel,
        grid=(b, l // tm, n // tn),
        in_specs=[
            pl.BlockSpec((1, tm, d), lambda b, i, j: (b, i, 0)),
            _cond_spec(shift),
            _cond_spec(scale),
            pl.BlockSpec((None, d, tn), lambda b, i, j: (layer, 0, j)),
        ],
        out_specs=pl.BlockSpec((1, tm, tn), lambda b, i, j: (b, i, j)),
        out_shape=jax.ShapeDtypeStruct((b, l, n), F32),
        scratch_shapes=[pltpu.VMEM((tm, d), BF16)],
        compiler_params=_cparams("parallel", "parallel", "arbitrary"),
        name="in_projection",
    )(x, shift, scale, w)


LOG2E = 1.4426950408889634
SCAN_HEADS_PER_STEP = 4
SCAN_STAGE_LAGS = (3, 2)
_HIGH_LEVELS = (8, 16, 32, 64)
_M_DIAG, _M_G1, _M_G2, _M_G4, _M_Q1, _M_K1, _M_Q2, _M_Q4, _M_H8, _M_H16, _M_H32 = range(11)
_R_Q1, _R_K1, _R_Q2, _R_Q4 = range(4)


def _scan_constants(reverse):
    nb = SCAN_BLOCK
    row = np.arange(nb)[:, None]
    col = np.arange(nb)[None, :]
    x = row ^ col
    tabs = np.zeros((11, nb, nb), np.float32)
    tabs[_M_DIAG] = x == 0
    tabs[_M_G1], tabs[_M_G2], tabs[_M_G4] = x < 2, x < 4, x < 8

    def q_role(m):
        upper = (row & m) == 0
        return np.broadcast_to(upper if reverse else ~upper, (nb, nb))

    tabs[_M_Q1], tabs[_M_Q2], tabs[_M_Q4] = q_role(1), q_role(2), q_role(4)
    tabs[_M_K1] = 1.0 - tabs[_M_Q1]
    for idx, m in ((_M_H8, 8), (_M_H16, 16), (_M_H32, 32)):
        grp = np.arange(nb // 2)[:, None] // m
        tabs[idx, :nb // 2] = (col // (2 * m)) == grp
    tri = (col >= row) if reverse else (col <= row)
    roles = jnp.asarray(tabs[[_M_Q1, _M_K1, _M_Q2, _M_Q4]], BF16)
    return jnp.asarray(tabs), roles, jnp.asarray(np.concatenate([tri, tri], axis=1), BF16)


def _dot_nt(a, b):
    return lax.dot_general(a, b, (((1,), (1,)), ((), ())), preferred_element_type=F32)


def _dot_tn(a, b):
    return lax.dot_general(a, b, (((0,), (0,)), ((), ())), preferred_element_type=F32)


def _scan_gates(zq, zf, lb, c_slot, tri2):
    q = _silu_tanh(zq)
    l2_lb = jnp.log2(lb)
    e = jnp.exp2(-LOG2E * jnp.abs(zf))
    a2 = jnp.minimum(zf, 0.0) * LOG2E - jnp.log2(1.0 + e) + jnp.log1p(-lb) * LOG2E
    g2 = jnp.maximum(a2, l2_lb) + jnp.log2(1.0 + jnp.exp2(-jnp.abs(a2 - l2_lb)))
    f = jnp.exp2(g2)
    k = 1.0 - f
    g_hi = g2.astype(BF16)
    g_lo = (g2 - g_hi.astype(F32)).astype(BF16)
    c2 = jnp.dot(tri2, jnp.concatenate([g_hi, g_lo], axis=0), preferred_element_type=F32)
    c_slot[...] = c2
    return q, k, f, c2


def _scan_scores(q, k, f, c2, c_slot, tab_ref, role_ref, *, reverse):
    nb = SCAN_BLOCK
    half = nb // 2
    n_tiles = nb // SUBLANES
    tile = lambda x, i: x[i * SUBLANES:(i + 1) * SUBLANES]
    qb, kb = q.astype(BF16), k.astype(BF16)

    def crow(r):
        return c_slot[r:r + 1, :]

    lvl = _dot_nt(qb, kb) * tab_ref[_M_DIAG]
    qt = qb * (f.astype(BF16) * role_ref[_R_Q1])
    kt = kb * role_ref[_R_K1]
    lvl = lvl + _dot_nt(qt, kt) * tab_ref[_M_G1]

    upper_half_tile = lax.broadcasted_iota(jnp.int32, (SUBLANES, LANES), 0) >= SUBLANES // 2
    for m, rq, mg in ((2, _R_Q2, _M_G2), (4, _R_Q4, _M_G4)):
        refs = []
        for t in range(n_tiles):
            r0 = t * SUBLANES + (m if reverse else m - 1)
            rb = jnp.broadcast_to(crow(r0), (SUBLANES, LANES))
            if m == 2:
                rb = jnp.where(upper_half_tile, jnp.broadcast_to(crow(r0 + 4), (SUBLANES, LANES)), rb)
            refs.append(rb)
        w = jnp.exp2(-jnp.abs(c2 - jnp.concatenate(refs, axis=0))).astype(BF16)
        wq = w * role_ref[rq]
        wk = w - wq
        lvl = lvl + _dot_nt(qb * wq, kb * wk) * tab_ref[mg]
    score_tiles = [tile(lvl, i) for i in range(n_tiles)]

    for m, mh in zip(_HIGH_LEVELS, (_M_H8, _M_H16, _M_H32, None)):
        packed = m % (2 * SUBLANES) == 0
        q_parts, k_parts, q_tiles = [], [], []
        for g in range(nb // (2 * m)):
            base = g * 2 * m
            first, second = slice(base, base + m), slice(base + m, base + 2 * m)
            q_rows, k_rows = (first, second) if reverse else (second, first)
            c_ref = crow(base + m if reverse else base + m - 1)
            wq = jnp.exp2(c2[q_rows] - c_ref)
            wk = jnp.exp2(c_ref - c2[k_rows])
            if packed:
                q_parts.append(qb[q_rows] * wq.astype(BF16))
                k_part = kb[k_rows] * wk.astype(BF16)
                zeros = jnp.zeros((m, LANES), BF16)
            else:
                q_parts.append(q[q_rows] * wq)
                k_part = k[k_rows] * wk
                zeros = jnp.zeros((m, LANES), F32)
            k_parts += [zeros, k_part] if reverse else [k_part, zeros]
            q_tiles += list(range(q_rows.start // SUBLANES, q_rows.stop // SUBLANES))
        res = _dot_nt(jnp.concatenate(q_parts, axis=0).astype(BF16),
                      jnp.concatenate(k_parts, axis=0).astype(BF16))
        if mh is not None:
            res = res * tab_ref[mh, 0:half, :]
        for j, t in enumerate(q_tiles):
            score_tiles[t] = score_tiles[t] + tile(res, j)
    return jnp.concatenate(score_tiles, axis=0), qb, kb


def _scan_kernel(*refs, reverse, finalize, n_sub):
    if finalize:
        (zq_ref, zv_ref, zf_ref, lb_ref, s0_ref, tab_ref, role_ref, tri_ref, zg_ref, oo_ref, ng_ref,
         o_ref, sfin_ref, s_scr, c_scr) = refs
    else:
        (zq_ref, zv_ref, zf_ref, lb_ref, s0_ref, tab_ref, role_ref, tri_ref,
         o_ref, sfin_ref, s_scr, c_scr) = refs
    nb = SCAN_BLOCK
    hps = SCAN_HEADS_PER_STEP
    ci = pl.program_id(2)

    @pl.when(ci == 0)
    def _():
        s_scr[...] = s0_ref[0]

    tri2 = tri_ref[...]
    states = [s_scr[hh] for hh in range(hps)]
    order = [(n_sub - 1 - i) if reverse else i for i in range(n_sub)]
    items = [(j, hh) for j in order for hh in range(hps)]
    where = {it: (slice(it[0] * nb, (it[0] + 1) * nb), slice(it[1] * LANES, (it[1] + 1) * LANES)) for it in items}
    slot = {it: c_scr.at[it[0] * hps + it[1]] for it in items}
    act, scores, packed = {}, {}, {}

    def stage_gates(it):
        rows, cols = where[it]
        act[it] = _scan_gates(zq_ref[0, rows, cols], zf_ref[0, rows, cols], lb_ref[0, :, cols], slot[it], tri2)

    def stage_scores(it):
        q, k, f, c2 = act[it]
        scores[it], *packed[it] = _scan_scores(q, k, f, c2, slot[it], tab_ref, role_ref, reverse=reverse)

    def stage_output(it):
        rows, cols = where[it]
        hh = it[1]
        _, _, _, c2 = act.pop(it)
        qb, kb = packed.pop(it)
        v = zv_ref[0, rows, cols].astype(BF16)
        c_tot = slot[it][0:1, :] if reverse else slot[it][nb - 1:nb, :]
        o = jnp.dot(scores.pop(it).astype(BF16), v, preferred_element_type=F32)
        o = o + _dot_nt(qb * jnp.exp2(c2).astype(BF16), states[hh].astype(BF16))
        states[hh] = states[hh] * jnp.exp2(c_tot) + _dot_tn(v, kb * jnp.exp2(c_tot - c2).astype(BF16))
        if finalize:
            o = o + oo_ref[0, rows, cols]
            ms = jnp.mean(o * o, axis=-1, keepdims=True)
            o = o * lax.rsqrt(ms + RMS_EPS) * ng_ref[0, :, cols] * _silu_tanh(zg_ref[0, rows, cols])
        o_ref[0, rows, cols] = o.astype(o_ref.dtype)

    lag_s, lag_o = SCAN_STAGE_LAGS
    for n in range(len(items) + lag_s + lag_o):
        if n < len(items):
            stage_gates(items[n])
        if 0 <= n - lag_s < len(items):
            stage_scores(items[n - lag_s])
        if 0 <= n - lag_s - lag_o < len(items):
            stage_output(items[n - lag_s - lag_o])
    for hh in range(hps):
        s_scr[hh] = states[hh]

    @pl.when(ci == pl.num_programs(2) - 1)
    def _():
        sfin_ref[0] = s_scr[...]


def _hgrn2_scan(z, col_q, col_v, col_f, lb, s0, *, reverse, final=None):
    b, l, _ = z.shape
    hps = SCAN_HEADS_PER_STEP
    heads, hd, wd = A_HEADS, LANES, SCAN_HEADS_PER_STEP * LANES
    cb = min(l, 512)
    nc = l // cb
    n_sub = cb // SCAN_BLOCK
    assert heads % hps == 0 and col_q % hps == 0 and col_v % hps == 0 and col_f % hps == 0

    def chunk(i):
        return (nc - 1 - i) if reverse else i

    def zspec(col0):
        return pl.BlockSpec((1, cb, wd), lambda b, h, i: (b, chunk(i), col0 // hps + h))

    def whole(a):
        return pl.BlockSpec(a.shape, lambda b, h, i: (0,) * a.ndim)

    head_row = pl.BlockSpec((1, 1, wd), lambda b, h, i: (h, 0, 0))
    state = pl.BlockSpec((1, hps, hd, hd), lambda b, h, i: (b, h, 0, 0))
    tabs, roles, tri = _scan_constants(reverse)
    in_specs = [zspec(col_q), zspec(col_v), zspec(col_f), head_row, state, whole(tabs), whole(roles), whole(tri)]
    args = [z, z, z, lb.reshape(heads // hps, 1, wd), s0, tabs, roles, tri]
    if final is not None:
        col_g, other, norm_g = final
        assert col_g % hps == 0
        in_specs += [zspec(col_g), zspec(0), head_row]
        args += [z, other, norm_g.reshape(heads // hps, 1, wd)]
    out_dtype = BF16 if final is not None else F32
    return pl.pallas_call(
        functools.partial(_scan_kernel, reverse=reverse, finalize=final is not None, n_sub=n_sub),
        grid=(b, heads // hps, nc),
        in_specs=in_specs,
        out_specs=[zspec(0), state],
        out_shape=[jax.ShapeDtypeStruct((b, l, heads * hd), out_dtype),
                   jax.ShapeDtypeStruct((b, heads, hd, hd), F32)],
        scratch_shapes=[pltpu.VMEM((hps, hd, hd), F32), pltpu.VMEM((n_sub * hps, SCAN_BLOCK, hd), F32)],
        compiler_params=_cparams("parallel", "parallel", "arbitrary"),
        name="hgrn2_scan_bwd" if reverse else "hgrn2_scan_fwd",
    )(*args)


def _dft_cos_sin(n):
    idx = np.arange(n, dtype=np.int64)
    ang = 2.0 * np.pi * ((idx[:, None] * idx[None, :]) % n).astype(np.float64) / n
    return np.cos(ang), np.sin(ang)


FOURIER_PITCH = LANES + SUBLANES


def _fourier_two_stage_kernel(z_ref, csc_ref, csq_ref, csr_ref, tw1_ref, o_ref, tre_scr, tim_scr, *, q):
    nr = LANES
    pitch = FOURIER_PITCH
    csc = csc_ref[...]
    csq = csq_ref[...]
    tw_c1 = tw1_ref[0]
    tw_s1 = tw1_ref[1]

    def stage_a(r, tw):
        tw_c, tw_s = tw
        x = z_ref[0, pl.ds(r, q, stride=nr), :]
        uv = jnp.dot(x.astype(BF16), csc, preferred_element_type=F32)
        res = jnp.dot(csq, uv.astype(BF16), preferred_element_type=F32)
        cu, cv = res[:q, :LANES], res[:q, LANES:]
        su, sv = res[q:, :LANES], res[q:, LANES:]
        g_re = cu - sv
        g_im = -(cv + su)
        tre_scr[pl.ds(r, q, stride=pitch), :] = g_re * tw_c + g_im * tw_s
        tim_scr[pl.ds(r, q, stride=pitch), :] = g_im * tw_c - g_re * tw_s
        return tw_c * tw_c1 - tw_s * tw_s1, tw_s * tw_c1 + tw_c * tw_s1

    lax.fori_loop(0, nr, stage_a, (jnp.ones((q, LANES), F32), jnp.zeros((q, LANES), F32)), unroll=8)

    csr = csr_ref[...]

    def stage_b(ka, carry):
        off = pl.multiple_of(ka * pitch, SUBLANES)
        t = jnp.concatenate([tre_scr[pl.ds(off, nr), :], tim_scr[pl.ds(off, nr), :]], axis=0)
        y = jnp.dot(csr, t.astype(BF16), preferred_element_type=F32)
        o_ref[0, pl.ds(ka, nr, stride=q), :] = y
        return carry

    lax.fori_loop(0, q, stage_b, 0, unroll=8)


def _fourier_direct_kernel(z_ref, csc_ref, csl_ref, o_ref):
    uv = jnp.dot(z_ref[0].astype(BF16), csc_ref[...], preferred_element_type=F32)
    t = jnp.concatenate([uv[:, :LANES], uv[:, LANES:]], axis=0)
    o_ref[0] = jnp.dot(csl_ref[...], t.astype(BF16), preferred_element_type=F32)


def _fourier_mixer(z, col0):
    b, l, _ = z.shape
    scale = 1.0 / math.sqrt(l * LANES)
    cc, sc = _dft_cos_sin(LANES)
    csc = jnp.asarray(np.concatenate([cc, sc], axis=1) * scale, BF16)
    out_shape = jax.ShapeDtypeStruct((b, l, F_GROUPS * LANES), F32)
    slab = pl.BlockSpec((1, l, LANES), lambda b, g: (b, 0, col0 + g))
    out_slab = pl.BlockSpec((1, l, LANES), lambda b, g: (b, 0, g))

    def whole(a):
        return pl.BlockSpec(a.shape, lambda b, g: (0,) * a.ndim)

    if l <= 1024:
        cl, sl = _dft_cos_sin(l)
        csl = jnp.asarray(np.concatenate([cl, -sl], axis=1), BF16)
        return pl.pallas_call(
            _fourier_direct_kernel,
            grid=(b, F_GROUPS),
            in_specs=[slab, whole(csc), whole(csl)],
            out_specs=out_slab,
            out_shape=out_shape,
            compiler_params=_cparams("parallel", "parallel"),
            name="fourier_direct",
        )(z, csc, csl)

    q = l // LANES
    cq, sq = _dft_cos_sin(q)
    csq = jnp.asarray(np.concatenate([cq, sq], axis=0), BF16)
    csr = jnp.asarray(np.concatenate([cc, sc], axis=1), BF16)
    ang1 = 2.0 * np.pi * np.arange(q, dtype=np.float64) / l
    tw1 = jnp.asarray(np.stack([np.broadcast_to(np.cos(ang1)[:, None], (q, LANES)),
                                np.broadcast_to(np.sin(ang1)[:, None], (q, LANES))]), F32)
    return pl.pallas_call(
        functools.partial(_fourier_two_stage_kernel, q=q),
        grid=(b, F_GROUPS),
        in_specs=[slab, whole(csc), whole(csq), whole(csr), whole(tw1)],
        out_specs=pl.BlockSpec((1, l, LANES), lambda b, g: (b, 0, g), pipeline_mode=pl.Buffered(1)),
        out_shape=out_shape,
        scratch_shapes=[pltpu.VMEM((q * FOURIER_PITCH, LANES), F32), pltpu.VMEM((q * FOURIER_PITCH, LANES), F32)],
        compiler_params=_cparams("parallel", "parallel"),
        name="fourier_two_stage",
    )(z, csc, csq, csr, tw1)


def _gelu(x):
    return jax.nn.gelu(x, approximate=True)


def _chunk_mlp_kernel(zu_ref, zv_ref, g_ref, b_ref, ws_ref, bs_ref, o_ref, *, n_chunks):
    v = _layer_norm(_gelu(zv_ref[0])) * g_ref[...] + b_ref[...]
    vb = v.astype(BF16)
    for c in range(n_chunks):
        rows = slice(c * MLP_CHUNK, (c + 1) * MLP_CHUNK)
        for g in range(C_GROUPS):
            cols = slice(g * LANES, (g + 1) * LANES)
            sv = jnp.dot(ws_ref[g], vb[rows, cols], preferred_element_type=F32) + bs_ref[g]
            o_ref[0, rows, cols] = (_gelu(zu_ref[0, rows, cols]) * sv).astype(o_ref.dtype)


def _chunk_mlp_mixer(z, col_u, col_v, ln_g, ln_b, w_s, b_s):
    b, l, _ = z.shape
    dc = C_GROUPS * LANES
    tm = min(l, 512)
    bs_full = jnp.broadcast_to(b_s[:, :, None], (C_GROUPS, MLP_CHUNK, LANES)).astype(F32)

    def whole(a):
        return pl.BlockSpec(a.shape, lambda b, i: (0,) * a.ndim)

    ws = w_s.astype(BF16)
    g2, b2 = ln_g.reshape(1, dc), ln_b.reshape(1, dc)
    return pl.pallas_call(
        functools.partial(_chunk_mlp_kernel, n_chunks=tm // MLP_CHUNK),
        grid=(b, l // tm),
        in_specs=[pl.BlockSpec((1, tm, dc), lambda b, i: (b, i, col_u)),
                  pl.BlockSpec((1, tm, dc), lambda b, i: (b, i, col_v)),
                  whole(g2), whole(b2), whole(ws), whole(bs_full)],
        out_specs=pl.BlockSpec((1, tm, dc), lambda b, i: (b, i, 0)),
        out_shape=jax.ShapeDtypeStruct((b, l, dc), BF16),
        compiler_params=_cparams("parallel", "parallel"),
        name="chunk_mlp_mixer",
    )(z, z, g2, b2, ws, bs_full)


OUTPROJ_SUBTILE = 128


def _outproj_kernel(oa_ref, yf_ref, yc_ref, w_ref, x_ref, gate_ref, lng_ref, lnb_ref,
                    sh_ref, sc_ref, x1_ref, h_ref, *, alpha):
    tm = x_ref.shape[1]
    sub = min(tm, OUTPROJ_SUBTILE)
    for r0 in range(0, tm, sub):
        rows = slice(r0, r0 + sub)
        mixed = jnp.concatenate([oa_ref[0, rows, :], yf_ref[0, rows, :].astype(BF16), yc_ref[0, rows, :]], axis=1)
        y = jnp.dot(mixed, w_ref[...], preferred_element_type=F32)
        x1 = _layer_norm(alpha * x_ref[0, rows, :] + gate_ref[0] * y) * lng_ref[...] + lnb_ref[...]
        x1_ref[0, rows, :] = x1
        h_ref[0, rows, :] = (_layer_norm(x1) * (1.0 + sc_ref[0]) + sh_ref[0]).astype(BF16)


def _out_projection(o_a, y_f, y_c, w_out, layer, x, gate, ln_g, ln_b, shift2, scale2, alpha):
    b, l, d = x.shape
    da, df, dc = o_a.shape[-1], y_f.shape[-1], y_c.shape[-1]
    tm = min(l, 512)
    row = lambda w: pl.BlockSpec((1, tm, w), lambda b, i: (b, i, 0))

    wspec = pl.BlockSpec((None, da + df + dc, d), lambda b, i: (layer, 0, 0), pipeline_mode=pl.Buffered(1))
    vec = pl.BlockSpec((1, d), lambda b, i: (0, 0))
    return pl.pallas_call(
        functools.partial(_outproj_kernel, alpha=alpha),
        grid=(b, l // tm),
        in_specs=[row(da), row(df), row(dc),
                  wspec,
                  row(d), _cond_spec(gate), vec, vec, _cond_spec(shift2), _cond_spec(scale2)],
        out_specs=[row(d), row(d)],
        out_shape=[jax.ShapeDtypeStruct((b, l, d), F32), jax.ShapeDtypeStruct((b, l, d), BF16)],
        compiler_params=_cparams("parallel", "parallel"),
        name="out_projection",
    )(o_a, y_f, y_c, w_out, x, gate, ln_g.reshape(1, d), ln_b.reshape(1, d), shift2, scale2)


FFN_SUBTILE = 256


def _ffn_up_kernel(*refs, width, vertical, tm):
    if vertical:
        h_ref, hp_ref, hn_ref, wg_ref, wu_ref, cw_ref, cb_ref, o_ref, g_scr = refs
    else:
        h_ref, wg_ref, wu_ref, cw_ref, cb_ref, o_ref, g_scr = refs
    i = pl.program_id(1)
    h = h_ref[0]
    halo = width if vertical else 0
    n_ext = tm + 2 * halo
    tf = o_ref.shape[2]
    sub = min(tf, FFN_SUBTILE)
    subtiles = [slice(c0, c0 + sub) for c0 in range(0, tf, sub)]
    for cols in subtiles:
        wg = wg_ref[:, cols]
        g_scr[halo:halo + tm, cols] = jnp.dot(h, wg, preferred_element_type=F32)
        if vertical:
            not_first = (i > 0).astype(F32)
            not_last = (i < pl.num_programs(1) - 1).astype(F32)
            g_scr[0:halo, cols] = jnp.dot(hp_ref[0], wg, preferred_element_type=F32) * not_first
            g_scr[halo + tm:, cols] = jnp.dot(hn_ref[0], wg, preferred_element_type=F32) * not_last
    colpos = lax.broadcasted_iota(jnp.int32, (n_ext, sub), 0) % width
    for cols in subtiles:
        ext = g_scr[:, cols]
        left = jnp.where(colpos == 0, 0.0, pltpu.roll(ext, 1, axis=0))
        right = jnp.where(colpos == width - 1, 0.0, pltpu.roll(ext, n_ext - 1, axis=0))
        taps = (left, ext, right)
        acc = None
        for dh in (range(3) if vertical else (1,)):
            base = halo + (dh - 1) * width
            for dw in range(3):
                term = taps[dw][base:base + tm, :] * cw_ref[dh * 3 + dw:dh * 3 + dw + 1, cols]
                acc = term if acc is None else acc + term
        gate = _silu(acc + cb_ref[:, cols])
        up = jnp.dot(h, wu_ref[:, cols], preferred_element_type=F32)
        o_ref[0, :, cols] = (gate * up).astype(o_ref.dtype)


def _ffn_up(h, w_gate, w_up, conv_w, conv_b, layer, width):
    b, l, d = h.shape
    depth, _, f = w_gate.shape
    vertical = l > width
    tm = min(l, 1024)
    tf = 512
    assert tm % width == 0 and f % tf == 0
    rows_per_tile = tm // width
    n_rows = l // width
    in_specs = [pl.BlockSpec((1, tm, d), lambda b, i, j: (b, i, 0))]
    args = [h]
    if vertical:
        in_specs += [
            pl.BlockSpec((1, width, d), lambda b, i, j: (b, jnp.maximum(i * rows_per_tile - 1, 0), 0)),
            pl.BlockSpec((1, width, d), lambda b, i, j: (b, jnp.minimum((i + 1) * rows_per_tile, n_rows - 1), 0)),
        ]
        args += [h, h]
    wspec = pl.BlockSpec((None, d, tf), lambda b, i, j: (layer, 0, j))
    in_specs += [wspec, wspec,
                 pl.BlockSpec((None, 9, tf), lambda b, i, j: (layer, 0, j)),
                 pl.BlockSpec((None, 1, tf), lambda b, i, j: (layer, 0, j))]
    args += [w_gate, w_up, conv_w.reshape(depth, 9, f), conv_b.reshape(depth, 1, f)]
    ext_rows = tm + (2 * width if vertical else 0)
    return pl.pallas_call(
        functools.partial(_ffn_up_kernel, width=width, vertical=vertical, tm=tm),
        grid=(b, l // tm, f // tf),
        in_specs=in_specs,
        out_specs=pl.BlockSpec((1, tm, tf), lambda b, i, j: (b, i, j)),
        out_shape=jax.ShapeDtypeStruct((b, l, f), BF16),
        scratch_shapes=[pltpu.VMEM((ext_rows, tf), F32)],
        compiler_params=_cparams("parallel", "parallel", "arbitrary"),
        name="ffn_up",
    )(*args)


def _ffn_down_kernel(hid_ref, w_ref, x_ref, gate_ref, lng_ref, lnb_ref, o_ref, *, alpha):
    f = jnp.dot(hid_ref[0], w_ref[...], preferred_element_type=F32)
    y = alpha * x_ref[0] + gate_ref[0] * f
    o_ref[0] = _layer_norm(y) * lng_ref[...] + lnb_ref[...]


def _ffn_down(hid, w_down, layer, x, gate, ln_g, ln_b, alpha):
    b, l, f = hid.shape
    d = x.shape[-1]
    tm = min(l, 512)
    vec = pl.BlockSpec((1, d), lambda b, i: (0, 0))
    return pl.pallas_call(
        functools.partial(_ffn_down_kernel, alpha=alpha),
        grid=(b, l // tm),
        in_specs=[pl.BlockSpec((1, tm, f), lambda b, i: (b, i, 0)),
                  pl.BlockSpec((None, f, d), lambda b, i: (layer, 0, 0), pipeline_mode=pl.Buffered(1)),
                  pl.BlockSpec((1, tm, d), lambda b, i: (b, i, 0)),
                  _cond_spec(gate), vec, vec],
        out_specs=pl.BlockSpec((1, tm, d), lambda b, i: (b, i, 0)),
        out_shape=jax.ShapeDtypeStruct((b, l, d), F32),
        compiler_params=_cparams("parallel", "parallel"),
        name="ffn_down",
    )(hid, w_down, x, gate, ln_g.reshape(1, d), ln_b.reshape(1, d))


def kernel(x, c, ctx, c_ctx, w_ada, b_ada, w_in, lower_bounds, a_norm_g, sg_norm_g, sg_norm_b,
           w_spatial, b_spatial, w_out, ln1_g, ln1_b, w_gate, w_up, conv_w, conv_b, w_down, ln2_g, ln2_b):
    batch, seq, d = x.shape
    depth = w_in.shape[0]
    d_a = a_norm_g.shape[-1]
    d_c = sg_norm_g.shape[-1]
    d_f = w_in.shape[-1] - 5 * d_a - 2 * d_c
    assert d_a == A_HEADS * LANES and d_c == C_GROUPS * LANES and d_f == F_GROUPS * LANES
    alpha = (2 * depth) ** 0.25
    a_blk = d_a // LANES

    lbs = jnp.cumsum(jax.nn.softmax(lower_bounds.astype(F32), axis=1), axis=1)
    lbs = lbs - lbs[:, :1]

    rows = -(-(batch + 1) // SUBLANES) * SUBLANES
    cond = jnp.zeros((rows, d), F32).at[:batch].set(c).at[batch].set(c_ctx)
    mods = _ada_modulation(cond, w_ada, b_ada)

    w_in_b, w_out_b = w_in.astype(BF16), w_out.astype(BF16)
    w_gate_b, w_up_b, w_down_b = w_gate.astype(BF16), w_up.astype(BF16), w_down.astype(BF16)
    zero_state = jnp.zeros((batch, A_HEADS, LANES, LANES), F32)

    def mixers(z, s_fwd, s_bwd, l, with_output=True):
        o_b, sf_b = _hgrn2_scan(z, 0, a_blk, 3 * a_blk, lbs[1, l], s_bwd, reverse=True)
        if not with_output:
            _, sf_f = _hgrn2_scan(z, 0, a_blk, 2 * a_blk, lbs[0, l], s_fwd, reverse=False)
            return None, sf_f, sf_b
        o_a, sf_f = _hgrn2_scan(z, 0, a_blk, 2 * a_blk, lbs[0, l], s_fwd, reverse=False,
                                final=(4 * a_blk, o_b, a_norm_g[l]))
        y_f = _fourier_mixer(z, 5 * a_blk)
        cu = (5 * d_a + d_f) // d_c
        y_c = _chunk_mlp_mixer(z, cu, cu + 1, sg_norm_g[l], sg_norm_b[l], w_spatial[l], b_spatial[l])
        return (o_a, y_f, y_c), sf_f, sf_b

    def rest_of_layer(xs, ys, m, l, width):
        sh2, sc2 = m[3], m[4]
        x1, h2 = _out_projection(*ys, w_out_b, l, xs, m[2], ln1_g[l], ln1_b[l], sh2, sc2, alpha)
        hid = _ffn_up(h2, w_gate_b, w_up_b, conv_w, conv_b, l, width)
        return _ffn_down(hid, w_down_b, l, x1, m[5], ln2_g[l], ln2_b[l], alpha)

    for l in range(depth):
        last = l == depth - 1
        m_all = [mods[l, :, i * d:(i + 1) * d] for i in range(6)]
        m_x = [t[:batch].reshape(batch, 1, d) for t in m_all]
        m_c = [t[batch:batch + 1].reshape(1, 1, d) for t in m_all]

        zc = _in_projection(ctx, m_c[0], m_c[1], w_in_b, l)
        ys_c, s_f, s_b = mixers(zc, zero_state, zero_state, l, with_output=not last)
        zx = _in_projection(x, m_x[0], m_x[1], w_in_b, l)
        ys_x, _, _ = mixers(zx, s_f, s_b, l)
        x = rest_of_layer(x, ys_x, m_x, l, GRID_W)
        if not last:
            ctx = rest_of_layer(ctx, ys_c, m_c, l, ctx.shape[1])
    return x
```

```python
import functools
import math

import numpy as np
import jax
import jax.numpy as jnp
from jax import lax
from jax.experimental import pallas as pl
from jax.experimental.pallas import tpu as pltpu

F32 = jnp.float32
BF16 = jnp.bfloat16

LANES = 128
SUBLANES = 8
VMEM_LIMIT_BYTES = 56 * 1024 * 1024

A_HEADS = 8
F_GROUPS = 4
C_GROUPS = 4
MLP_CHUNK = 128
SCAN_BLOCK = 128
GRID_W = 64
LN_EPS = 1e-6
RMS_EPS = 1e-6


def _cparams(*sem):
    return pltpu.CompilerParams(dimension_semantics=sem, vmem_limit_bytes=VMEM_LIMIT_BYTES)


def _silu(x):
    return x * jax.nn.sigmoid(x)


def _silu_tanh(x):
    hx = 0.5 * x
    return hx + hx * jnp.tanh(hx)


def _layer_norm(x):
    mu = jnp.mean(x, axis=-1, keepdims=True)
    xc = x - mu
    var = jnp.mean(xc * xc, axis=-1, keepdims=True)
    return xc * lax.rsqrt(var + LN_EPS)


def _pick_tile(n, cap, quantum=LANES):
    return max(t for t in range(quantum, cap + 1, quantum) if n % t == 0)


def _cond_spec(arr):
    d = arr.shape[-1]
    if arr.shape[0] == 1:
        return pl.BlockSpec((1, 1, d), lambda b, *_: (0, 0, 0))
    return pl.BlockSpec((1, 1, d), lambda b, *_: (b, 0, 0))


def _ada_kernel(cond_ref, w_ref, b_ref, o_ref):
    s = _silu(cond_ref[...]).astype(BF16)
    o_ref[0] = jnp.dot(s, w_ref[0].astype(BF16), preferred_element_type=F32) + b_ref[0]


def _ada_modulation(cond, w_ada, b_ada):
    depth, d, n = w_ada.shape
    rows = cond.shape[0]
    tn = 1024
    return pl.pallas_call(
        _ada_kernel,
        grid=(depth, n // tn),
        in_specs=[
            pl.BlockSpec((rows, d), lambda l, j: (0, 0)),
            pl.BlockSpec((1, d, tn), lambda l, j: (l, 0, j)),
            pl.BlockSpec((1, 1, tn), lambda l, j: (l, 0, j)),
        ],
        out_specs=pl.BlockSpec((1, rows, tn), lambda l, j: (l, 0, j)),
        out_shape=jax.ShapeDtypeStruct((depth, rows, n), F32),
        compiler_params=_cparams("parallel", "parallel"),
        name="ada_modulation",
    )(cond, w_ada, b_ada.reshape(depth, 1, n))


def _inproj_kernel(x_ref, sh_ref, sc_ref, w_ref, o_ref, h_scr):
    @pl.when(pl.program_id(2) == 0)
    def _():
        h = _layer_norm(x_ref[0]) * (1.0 + sc_ref[0]) + sh_ref[0]
        h_scr[...] = h.astype(BF16)

    o_ref[0] = jnp.dot(h_scr[...], w_ref[...], preferred_element_type=F32)


def _in_projection(x, shift, scale, w, layer):
    b, l, d = x.shape
    n = w.shape[2]
    tm = min(l, 1024)
    tn = _pick_tile(n, 1792)
    return pl.pallas_call(
        _inproj_kernel,
        grid=(b, l // tm, n // tn),
        in_specs=[
            pl.BlockSpec((1, tm, d), lambda b, i, j: (b, i, 0)),
            _cond_spec(shift),
            _cond_spec(scale),
            pl.BlockSpec((None, d, tn), lambda b, i, j: (layer, 0, j)),
        ],
        out_specs=pl.BlockSpec((1, tm, tn), lambda b, i, j: (b, i, j)),
        out_shape=jax.ShapeDtypeStruct((b, l, n), F32),
        scratch_shapes=[pltpu.VMEM((tm, d), BF16)],
        compiler_params=_cparams("parallel", "parallel", "arbitrary"),
        name="in_projection",
    )(x, shift, scale, w)


LOG2E = 1.4426950408889634
SCAN_HEADS_PER_STEP = 8
SCAN_STAGE_LAGS = (3, 2)
_HIGH_LEVELS = (8, 16, 32, 64)
_M_DIAG, _M_G1, _M_G2, _M_G4, _M_Q1, _M_K1, _M_Q2, _M_Q4, _M_H8, _M_H16, _M_H32 = range(11)
_R_Q1, _R_K1, _R_Q2, _R_Q4 = range(4)


def _scan_constants(reverse):
    nb = SCAN_BLOCK
    row = np.arange(nb)[:, None]
    col = np.arange(nb)[None, :]
    x = row ^ col
    tabs = np.zeros((11, nb, nb), np.float32)
    tabs[_M_DIAG] = x == 0
    tabs[_M_G1], tabs[_M_G2], tabs[_M_G4] = x < 2, x < 4, x < 8

    def q_role(m):
        upper = (row & m) == 0
        return np.broadcast_to(upper if reverse else ~upper, (nb, nb))

    tabs[_M_Q1], tabs[_M_Q2], tabs[_M_Q4] = q_role(1), q_role(2), q_role(4)
    tabs[_M_K1] = 1.0 - tabs[_M_Q1]
    for idx, m in ((_M_H8, 8), (_M_H16, 16), (_M_H32, 32)):
        grp = np.arange(nb // 2)[:, None] // m
        tabs[idx, :nb // 2] = (col // (2 * m)) == grp
    tri = (col >= row) if reverse else (col <= row)
    roles = jnp.asarray(tabs[[_M_Q1, _M_K1, _M_Q2, _M_Q4]], BF16)
    return jnp.asarray(tabs), roles, jnp.asarray(np.concatenate([tri, tri], axis=1), BF16)


def _dot_nt(a, b):
    return lax.dot_general(a, b, (((1,), (1,)), ((), ())), preferred_element_type=F32)


def _dot_tn(a, b):
    return lax.dot_general(a, b, (((0,), (0,)), ((), ())), preferred_element_type=F32)


def _scan_gates(zq, zf, lb, c_slot, tri2):
    q = _silu_tanh(zq)
    l2_lb = jnp.log2(lb)
    e = jnp.exp2(-LOG2E * jnp.abs(zf))
    a2 = jnp.minimum(zf, 0.0) * LOG2E - jnp.log2(1.0 + e) + jnp.log1p(-lb) * LOG2E
    g2 = jnp.maximum(a2, l2_lb) + jnp.log2(1.0 + jnp.exp2(-jnp.abs(a2 - l2_lb)))
    f = jnp.exp2(g2)
    k = 1.0 - f
    g_hi = g2.astype(BF16)
    g_lo = (g2 - g_hi.astype(F32)).astype(BF16)
    c2 = jnp.dot(tri2, jnp.concatenate([g_hi, g_lo], axis=0), preferred_element_type=F32)
    c_slot[...] = c2
    return q, k, f, c2


def _scan_scores(q, k, f, c2, c_slot, tab_ref, role_ref, *, reverse):
    nb = SCAN_BLOCK
    half = nb // 2
    n_tiles = nb // SUBLANES
    tile = lambda x, i: x[i * SUBLANES:(i + 1) * SUBLANES]
    qb, kb = q.astype(BF16), k.astype(BF16)

    def crow(r):
        return c_slot[r:r + 1, :]

    lvl = _dot_nt(qb, kb) * tab_ref[_M_DIAG]
    qt = qb * (f.astype(BF16) * role_ref[_R_Q1])
    kt = kb * role_ref[_R_K1]
    lvl = lvl + _dot_nt(qt, kt) * tab_ref[_M_G1]

    upper_half_tile = lax.broadcasted_iota(jnp.int32, (SUBLANES, LANES), 0) >= SUBLANES // 2
    for m, rq, mg in ((2, _R_Q2, _M_G2), (4, _R_Q4, _M_G4)):
        refs = []
        for t in range(n_tiles):
            r0 = t * SUBLANES + (m if reverse else m - 1)
            rb = jnp.broadcast_to(crow(r0), (SUBLANES, LANES))
            if m == 2:
                rb = jnp.where(upper_half_tile, jnp.broadcast_to(crow(r0 + 4), (SUBLANES, LANES)), rb)
            refs.append(rb)
        w = jnp.exp2(-jnp.abs(c2 - jnp.concatenate(refs, axis=0))).astype(BF16)
        wq = w * role_ref[rq]
        wk = w - wq
        lvl = lvl + _dot_nt(qb * wq, kb * wk) * tab_ref[mg]
    score_tiles = [tile(lvl, i) for i in range(n_tiles)]

    for m, mh in zip(_HIGH_LEVELS, (_M_H8, _M_H16, _M_H32, None)):
        packed = m % (2 * SUBLANES) == 0
        q_parts, k_parts, q_tiles = [], [], []
        for g in range(nb // (2 * m)):
            base = g * 2 * m
            first, second = slice(base, base + m), slice(base + m, base + 2 * m)
            q_rows, k_rows = (first, second) if reverse else (second, first)
            c_ref = crow(base + m if reverse else base + m - 1)
            wq = jnp.exp2(c2[q_rows] - c_ref)
            wk = jnp.exp2(c_ref - c2[k_rows])
            if packed:
                q_parts.append(qb[q_rows] * wq.astype(BF16))
                k_part = kb[k_rows] * wk.astype(BF16)
                zeros = jnp.zeros((m, LANES), BF16)
            else:
                q_parts.append(q[q_rows] * wq)
                k_part = k[k_rows] * wk
                zeros = jnp.zeros((m, LANES), F32)
            k_parts += [zeros, k_part] if reverse else [k_part, zeros]
            q_tiles += list(range(q_rows.start // SUBLANES, q_rows.stop // SUBLANES))
        res = _dot_nt(jnp.concatenate(q_parts, axis=0).astype(BF16),
                      jnp.concatenate(k_parts, axis=0).astype(BF16))
        if mh is not None:
            res = res * tab_ref[mh, 0:half, :]
        for j, t in enumerate(q_tiles):
            score_tiles[t] = score_tiles[t] + tile(res, j)
    return jnp.concatenate(score_tiles, axis=0), qb, kb


def _scan_kernel(*refs, reverse, finalize, n_sub):
    if finalize:
        (zq_ref, zv_ref, zf_ref, lb_ref, s0_ref, tab_ref, role_ref, tri_ref, zg_ref, oo_ref, ng_ref,
         o_ref, sfin_ref, s_scr, c_scr) = refs
    else:
        (zq_ref, zv_ref, zf_ref, lb_ref, s0_ref, tab_ref, role_ref, tri_ref,
         o_ref, sfin_ref, s_scr, c_scr) = refs
    nb = SCAN_BLOCK
    hps = SCAN_HEADS_PER_STEP
    ci = pl.program_id(2)

    @pl.when(ci == 0)
    def _():
        s_scr[...] = s0_ref[0]

    tri2 = tri_ref[...]
    states = [s_scr[hh] for hh in range(hps)]
    order = [(n_sub - 1 - i) if reverse else i for i in range(n_sub)]
    items = [(j, hh) for j in order for hh in range(hps)]
    where = {it: (slice(it[0] * nb, (it[0] + 1) * nb), slice(it[1] * LANES, (it[1] + 1) * LANES)) for it in items}
    slot = {it: c_scr.at[it[0] * hps + it[1]] for it in items}
    act, scores, packed = {}, {}, {}

    def stage_gates(it):
        rows, cols = where[it]
        act[it] = _scan_gates(zq_ref[0, rows, cols], zf_ref[0, rows, cols], lb_ref[0, :, cols], slot[it], tri2)

    def stage_scores(it):
        q, k, f, c2 = act[it]
        scores[it], *packed[it] = _scan_scores(q, k, f, c2, slot[it], tab_ref, role_ref, reverse=reverse)

    def stage_output(it):
        rows, cols = where[it]
        hh = it[1]
        _, _, _, c2 = act.pop(it)
        qb, kb = packed.pop(it)
        v = zv_ref[0, rows, cols].astype(BF16)
        c_tot = slot[it][0:1, :] if reverse else slot[it][nb - 1:nb, :]
        o = jnp.dot(scores.pop(it).astype(BF16), v, preferred_element_type=F32)
        o = o + _dot_nt(qb * jnp.exp2(c2).astype(BF16), states[hh].astype(BF16))
        states[hh] = states[hh] * jnp.exp2(c_tot) + _dot_tn(v, kb * jnp.exp2(c_tot - c2).astype(BF16))
        if finalize:
            o = o + oo_ref[0, rows, cols]
            ms = jnp.mean(o * o, axis=-1, keepdims=True)
            o = o * lax.rsqrt(ms + RMS_EPS) * ng_ref[0, :, cols] * _silu_tanh(zg_ref[0, rows, cols])
        o_ref[0, rows, cols] = o.astype(o_ref.dtype)

    lag_s, lag_o = SCAN_STAGE_LAGS
    for n in range(len(items) + lag_s + lag_o):
        if n < len(items):
            stage_gates(items[n])
        if 0 <= n - lag_s < len(items):
            stage_scores(items[n - lag_s])
        if 0 <= n - lag_s - lag_o < len(items):
            stage_output(items[n - lag_s - lag_o])
    for hh in range(hps):
        s_scr[hh] = states[hh]

    @pl.when(ci == pl.num_programs(2) - 1)
    def _():
        sfin_ref[0] = s_scr[...]


def _hgrn2_scan(z, col_q, col_v, col_f, lb, s0, *, reverse, final=None):
    b, l, _ = z.shape
    hps = SCAN_HEADS_PER_STEP
    heads, hd, wd = A_HEADS, LANES, SCAN_HEADS_PER_STEP * LANES
    cb = min(l, 256)
    nc = l // cb
    n_sub = cb // SCAN_BLOCK
    assert heads % hps == 0 and col_q % hps == 0 and col_v % hps == 0 and col_f % hps == 0

    def chunk(i):
        return (nc - 1 - i) if reverse else i

    def zspec(col0):
        return pl.BlockSpec((1, cb, wd), lambda b, h, i: (b, chunk(i), col0 // hps + h))

    def whole(a):
        return pl.BlockSpec(a.shape, lambda b, h, i: (0,) * a.ndim)

    head_row = pl.BlockSpec((1, 1, wd), lambda b, h, i: (h, 0, 0))
    state = pl.BlockSpec((1, hps, hd, hd), lambda b, h, i: (b, h, 0, 0))
    tabs, roles, tri = _scan_constants(reverse)
    in_specs = [zspec(col_q), zspec(col_v), zspec(col_f), head_row, state, whole(tabs), whole(roles), whole(tri)]
    args = [z, z, z, lb.reshape(heads // hps, 1, wd), s0, tabs, roles, tri]
    if final is not None:
        col_g, other, norm_g = final
        assert col_g % hps == 0
        in_specs += [zspec(col_g), zspec(0), head_row]
        args += [z, other, norm_g.reshape(heads // hps, 1, wd)]
    out_dtype = BF16 if final is not None else F32
    return pl.pallas_call(
        functools.partial(_scan_kernel, reverse=reverse, finalize=final is not None, n_sub=n_sub),
        grid=(b, heads // hps, nc),
        in_specs=in_specs,
        out_specs=[zspec(0), state],
        out_shape=[jax.ShapeDtypeStruct((b, l, heads * hd), out_dtype),
                   jax.ShapeDtypeStruct((b, heads, hd, hd), F32)],
        scratch_shapes=[pltpu.VMEM((hps, hd, hd), F32), pltpu.VMEM((n_sub * hps, SCAN_BLOCK, hd), F32)],
        compiler_params=_cparams("parallel", "parallel", "arbitrary"),
        name="hgrn2_scan_bwd" if reverse else "hgrn2_scan_fwd",
    )(*args)


def _dft_cos_sin(n):
    idx = np.arange(n, dtype=np.int64)
    ang = 2.0 * np.pi * ((idx[:, None] * idx[None, :]) % n).astype(np.float64) / n
    return np.cos(ang), np.sin(ang)


FOURIER_PITCH = LANES + SUBLANES


def _fourier_two_stage_kernel(z_ref, csc_ref, csq_ref, csr_ref, tw1_ref, o_ref, tre_scr, tim_scr, *, q):
    nr = LANES
    pitch = FOURIER_PITCH
    csc = csc_ref[...]
    csq = csq_ref[...]
    tw_c1 = tw1_ref[0]
    tw_s1 = tw1_ref[1]

    def stage_a(r, tw):
        tw_c, tw_s = tw
        x = z_ref[0, pl.ds(r, q, stride=nr), :]
        uv = jnp.dot(x.astype(BF16), csc, preferred_element_type=F32)
        res = jnp.dot(csq, uv.astype(BF16), preferred_element_type=F32)
        cu, cv = res[:q, :LANES], res[:q, LANES:]
        su, sv = res[q:, :LANES], res[q:, LANES:]
        g_re = cu - sv
        g_im = -(cv + su)
        tre_scr[pl.ds(r, q, stride=pitch), :] = g_re * tw_c + g_im * tw_s
        tim_scr[pl.ds(r, q, stride=pitch), :] = g_im * tw_c - g_re * tw_s
        return tw_c * tw_c1 - tw_s * tw_s1, tw_s * tw_c1 + tw_c * tw_s1

    lax.fori_loop(0, nr, stage_a, (jnp.ones((q, LANES), F32), jnp.zeros((q, LANES), F32)), unroll=8)

    csr = csr_ref[...]

    def stage_b(ka, carry):
        off = pl.multiple_of(ka * pitch, SUBLANES)
        t = jnp.concatenate([tre_scr[pl.ds(off, nr), :], tim_scr[pl.ds(off, nr), :]], axis=0)
        y = jnp.dot(csr, t.astype(BF16), preferred_element_type=F32)
        o_ref[0, pl.ds(ka, nr, stride=q), :] = y
        return carry

    lax.fori_loop(0, q, stage_b, 0, unroll=8)


def _fourier_direct_kernel(z_ref, csc_ref, csl_ref, o_ref):
    uv = jnp.dot(z_ref[0].astype(BF16), csc_ref[...], preferred_element_type=F32)
    t = jnp.concatenate([uv[:, :LANES], uv[:, LANES:]], axis=0)
    o_ref[0] = jnp.dot(csl_ref[...], t.astype(BF16), preferred_element_type=F32)


def _fourier_mixer(z, col0):
    b, l, _ = z.shape
    scale = 1.0 / math.sqrt(l * LANES)
    cc, sc = _dft_cos_sin(LANES)
    csc = jnp.asarray(np.concatenate([cc, sc], axis=1) * scale, BF16)
    out_shape = jax.ShapeDtypeStruct((b, l, F_GROUPS * LANES), F32)
    slab = pl.BlockSpec((1, l, LANES), lambda b, g: (b, 0, col0 + g))
    out_slab = pl.BlockSpec((1, l, LANES), lambda b, g: (b, 0, g))

    def whole(a):
        return pl.BlockSpec(a.shape, lambda b, g: (0,) * a.ndim)

    if l <= 1024:
        cl, sl = _dft_cos_sin(l)
        csl = jnp.asarray(np.concatenate([cl, -sl], axis=1), BF16)
        return pl.pallas_call(
            _fourier_direct_kernel,
            grid=(b, F_GROUPS),
            in_specs=[slab, whole(csc), whole(csl)],
            out_specs=out_slab,
            out_shape=out_shape,
            compiler_params=_cparams("parallel", "parallel"),
            name="fourier_direct",
        )(z, csc, csl)

    q = l // LANES
    cq, sq = _dft_cos_sin(q)
    csq = jnp.asarray(np.concatenate([cq, sq], axis=0), BF16)
    csr = jnp.asarray(np.concatenate([cc, sc], axis=1), BF16)
    ang1 = 2.0 * np.pi * np.arange(q, dtype=np.float64) / l
    tw1 = jnp.asarray(np.stack([np.broadcast_to(np.cos(ang1)[:, None], (q, LANES)),
                                np.broadcast_to(np.sin(ang1)[:, None], (q, LANES))]), F32)
    return pl.pallas_call(
        functools.partial(_fourier_two_stage_kernel, q=q),
        grid=(b, F_GROUPS),
        in_specs=[slab, whole(csc), whole(csq), whole(csr), whole(tw1)],
        out_specs=out_slab,
        out_shape=out_shape,
        scratch_shapes=[pltpu.VMEM((q * FOURIER_PITCH, LANES), F32), pltpu.VMEM((q * FOURIER_PITCH, LANES), F32)],
        compiler_params=_cparams("parallel", "parallel"),
        name="fourier_two_stage",
    )(z, csc, csq, csr, tw1)


def _gelu(x):
    return jax.nn.gelu(x, approximate=True)


def _chunk_mlp_kernel(zu_ref, zv_ref, g_ref, b_ref, ws_ref, bs_ref, o_ref, *, n_chunks):
    v = _layer_norm(_gelu(zv_ref[0])) * g_ref[...] + b_ref[...]
    vb = v.astype(BF16)
    for c in range(n_chunks):
        rows = slice(c * MLP_CHUNK, (c + 1) * MLP_CHUNK)
        for g in range(C_GROUPS):
            cols = slice(g * LANES, (g + 1) * LANES)
            sv = jnp.dot(ws_ref[g], vb[rows, cols], preferred_element_type=F32) + bs_ref[g]
            o_ref[0, rows, cols] = (_gelu(zu_ref[0, rows, cols]) * sv).astype(o_ref.dtype)


def _chunk_mlp_mixer(z, col_u, col_v, ln_g, ln_b, w_s, b_s):
    b, l, _ = z.shape
    dc = C_GROUPS * LANES
    tm = min(l, 512)
    bs_full = jnp.broadcast_to(b_s[:, :, None], (C_GROUPS, MLP_CHUNK, LANES)).astype(F32)

    def whole(a):
        return pl.BlockSpec(a.shape, lambda b, i: (0,) * a.ndim)

    ws = w_s.astype(BF16)
    g2, b2 = ln_g.reshape(1, dc), ln_b.reshape(1, dc)
    return pl.pallas_call(
        functools.partial(_chunk_mlp_kernel, n_chunks=tm // MLP_CHUNK),
        grid=(b, l // tm),
        in_specs=[pl.BlockSpec((1, tm, dc), lambda b, i: (b, i, col_u)),
                  pl.BlockSpec((1, tm, dc), lambda b, i: (b, i, col_v)),
                  whole(g2), whole(b2), whole(ws), whole(bs_full)],
        out_specs=pl.BlockSpec((1, tm, dc), lambda b, i: (b, i, 0)),
        out_shape=jax.ShapeDtypeStruct((b, l, dc), BF16),
        compiler_params=_cparams("parallel", "parallel"),
        name="chunk_mlp_mixer",
    )(z, z, g2, b2, ws, bs_full)


OUTPROJ_SUBTILE = 128


def _outproj_kernel(oa_ref, yf_ref, yc_ref, w_ref, x_ref, gate_ref, lng_ref, lnb_ref,
                    sh_ref, sc_ref, x1_ref, h_ref, *, alpha):
    tm = x_ref.shape[1]
    sub = min(tm, OUTPROJ_SUBTILE)
    for r0 in range(0, tm, sub):
        rows = slice(r0, r0 + sub)
        mixed = jnp.concatenate([oa_ref[0, rows, :], yf_ref[0, rows, :].astype(BF16), yc_ref[0, rows, :]], axis=1)
        y = jnp.dot(mixed, w_ref[...], preferred_element_type=F32)
        x1 = _layer_norm(alpha * x_ref[0, rows, :] + gate_ref[0] * y) * lng_ref[...] + lnb_ref[...]
        x1_ref[0, rows, :] = x1
        h_ref[0, rows, :] = (_layer_norm(x1) * (1.0 + sc_ref[0]) + sh_ref[0]).astype(BF16)


def _out_projection(o_a, y_f, y_c, w_out, layer, x, gate, ln_g, ln_b, shift2, scale2, alpha):
    b, l, d = x.shape
    da, df, dc = o_a.shape[-1], y_f.shape[-1], y_c.shape[-1]
    tm = min(l, 512)
    row = lambda w: pl.BlockSpec((1, tm, w), lambda b, i: (b, i, 0))

    wspec = pl.BlockSpec((None, da + df + dc, d), lambda b, i: (layer, 0, 0), pipeline_mode=pl.Buffered(1))
    vec = pl.BlockSpec((1, d), lambda b, i: (0, 0))
    return pl.pallas_call(
        functools.partial(_outproj_kernel, alpha=alpha),
        grid=(b, l // tm),
        in_specs=[row(da), row(df), row(dc),
                  wspec,
                  row(d), _cond_spec(gate), vec, vec, _cond_spec(shift2), _cond_spec(scale2)],
        out_specs=[row(d), row(d)],
        out_shape=[jax.ShapeDtypeStruct((b, l, d), F32), jax.ShapeDtypeStruct((b, l, d), BF16)],
        compiler_params=_cparams("parallel", "parallel"),
        name="out_projection",
    )(o_a, y_f, y_c, w_out, x, gate, ln_g.reshape(1, d), ln_b.reshape(1, d), shift2, scale2)


FFN_SUBTILE = 256


def _ffn_up_kernel(*refs, width, vertical, tm):
    if vertical:
        h_ref, hp_ref, hn_ref, wg_ref, wu_ref, cw_ref, cb_ref, o_ref, g_scr = refs
    else:
        h_ref, wg_ref, wu_ref, cw_ref, cb_ref, o_ref, g_scr = refs
    i = pl.program_id(1)
    h = h_ref[0]
    halo = width if vertical else 0
    n_ext = tm + 2 * halo
    tf = o_ref.shape[2]
    sub = min(tf, FFN_SUBTILE)
    subtiles = [slice(c0, c0 + sub) for c0 in range(0, tf, sub)]
    for cols in subtiles:
        wg = wg_ref[:, cols]
        g_scr[halo:halo + tm, cols] = jnp.dot(h, wg, preferred_element_type=F32)
        if vertical:
            not_first = (i > 0).astype(F32)
            not_last = (i < pl.num_programs(1) - 1).astype(F32)
            g_scr[0:halo, cols] = jnp.dot(hp_ref[0], wg, preferred_element_type=F32) * not_first
            g_scr[halo + tm:, cols] = jnp.dot(hn_ref[0], wg, preferred_element_type=F32) * not_last
    n_grid_rows = n_ext // width
    sublane = lax.broadcasted_iota(jnp.int32, (n_grid_rows, SUBLANES, sub), 1)

    def zero_column(x, first):
        x3 = x.reshape(n_grid_rows, width, sub)
        if first:
            edge = jnp.where(sublane == 0, 0.0, x3[:, :SUBLANES, :])
            x3 = jnp.concatenate([edge, x3[:, SUBLANES:, :]], axis=1)
        else:
            edge = jnp.where(sublane == SUBLANES - 1, 0.0, x3[:, width - SUBLANES:, :])
            x3 = jnp.concatenate([x3[:, :width - SUBLANES, :], edge], axis=1)
        return x3.reshape(n_ext, sub)

    for cols in subtiles:
        ext = g_scr[:, cols]
        left = zero_column(pltpu.roll(ext, 1, axis=0), True)
        right = zero_column(pltpu.roll(ext, n_ext - 1, axis=0), False)
        taps = (left, ext, right)
        acc = None
        for dh in (range(3) if vertical else (1,)):
            base = halo + (dh - 1) * width
            for dw in range(3):
                term = taps[dw][base:base + tm, :] * cw_ref[dh * 3 + dw:dh * 3 + dw + 1, cols]
                acc = term if acc is None else acc + term
        gate = _silu_tanh(acc + cb_ref[:, cols])
        up = jnp.dot(h, wu_ref[:, cols], preferred_element_type=F32)
        o_ref[0, :, cols] = (gate * up).astype(o_ref.dtype)


def _ffn_up(h, w_gate, w_up, conv_w, conv_b, layer, width):
    b, l, d = h.shape
    depth, _, f = w_gate.shape
    vertical = l > width
    tm = min(l, 1024)
    tf = 512
    assert tm % width == 0 and f % tf == 0
    rows_per_tile = tm // width
    n_rows = l // width
    in_specs = [pl.BlockSpec((1, tm, d), lambda b, i, j: (b, i, 0))]
    args = [h]
    if vertical:
        in_specs += [
            pl.BlockSpec((1, width, d), lambda b, i, j: (b, jnp.maximum(i * rows_per_tile - 1, 0), 0)),
            pl.BlockSpec((1, width, d), lambda b, i, j: (b, jnp.minimum((i + 1) * rows_per_tile, n_rows - 1), 0)),
        ]
        args += [h, h]
    wspec = pl.BlockSpec((None, d, tf), lambda b, i, j: (layer, 0, j))
    in_specs += [wspec, wspec,
                 pl.BlockSpec((None, 9, tf), lambda b, i, j: (layer, 0, j)),
                 pl.BlockSpec((None, 1, tf), lambda b, i, j: (layer, 0, j))]
    args += [w_gate, w_up, conv_w.reshape(depth, 9, f), conv_b.reshape(depth, 1, f)]
    ext_rows = tm + (2 * width if vertical else 0)
    return pl.pallas_call(
        functools.partial(_ffn_up_kernel, width=width, vertical=vertical, tm=tm),
        grid=(b, l // tm, f // tf),
        in_specs=in_specs,
        out_specs=pl.BlockSpec((1, tm, tf), lambda b, i, j: (b, i, j)),
        out_shape=jax.ShapeDtypeStruct((b, l, f), BF16),
        scratch_shapes=[pltpu.VMEM((ext_rows, tf), F32)],
        compiler_params=_cparams("parallel", "parallel", "arbitrary"),
        name="ffn_up",
    )(*args)


def _ffn_down_kernel(hid_ref, w_ref, x_ref, gate_ref, lng_ref, lnb_ref, o_ref, *, alpha):
    f = jnp.dot(hid_ref[0], w_ref[...], preferred_element_type=F32)
    y = alpha * x_ref[0] + gate_ref[0] * f
    o_ref[0] = _layer_norm(y) * lng_ref[...] + lnb_ref[...]


def _ffn_down(hid, w_down, layer, x, gate, ln_g, ln_b, alpha):
    b, l, f = hid.shape
    d = x.shape[-1]
    tm = min(l, 512)
    vec = pl.BlockSpec((1, d), lambda b, i: (0, 0))
    return pl.pallas_call(
        functools.partial(_ffn_down_kernel, alpha=alpha),
        grid=(b, l // tm),
        in_specs=[pl.BlockSpec((1, tm, f), lambda b, i: (b, i, 0)),
                  pl.BlockSpec((None, f, d), lambda b, i: (layer, 0, 0), pipeline_mode=pl.Buffered(1)),
                  pl.BlockSpec((1, tm, d), lambda b, i: (b, i, 0)),
                  _cond_spec(gate), vec, vec],
        out_specs=pl.BlockSpec((1, tm, d), lambda b, i: (b, i, 0)),
        out_shape=jax.ShapeDtypeStruct((b, l, d), F32),
        compiler_params=_cparams("parallel", "parallel"),
        name="ffn_down",
    )(hid, w_down, x, gate, ln_g.reshape(1, d), ln_b.reshape(1, d))


def kernel(x, c, ctx, c_ctx, w_ada, b_ada, w_in, lower_bounds, a_norm_g, sg_norm_g, sg_norm_b,
           w_spatial, b_spatial, w_out, ln1_g, ln1_b, w_gate, w_up, conv_w, conv_b, w_down, ln2_g, ln2_b):
    batch, seq, d = x.shape
    depth = w_in.shape[0]
    d_a = a_norm_g.shape[-1]
    d_c = sg_norm_g.shape[-1]
    d_f = w_in.shape[-1] - 5 * d_a - 2 * d_c
    assert d_a == A_HEADS * LANES and d_c == C_GROUPS * LANES and d_f == F_GROUPS * LANES
    alpha = (2 * depth) ** 0.25
    a_blk = d_a // LANES

    lbs = jnp.cumsum(jax.nn.softmax(lower_bounds.astype(F32), axis=1), axis=1)
    lbs = lbs - lbs[:, :1]

    rows = -(-(batch + 1) // SUBLANES) * SUBLANES
    cond = jnp.zeros((rows, d), F32).at[:batch].set(c).at[batch].set(c_ctx)
    mods = _ada_modulation(cond, w_ada, b_ada)

    w_in_b, w_out_b = w_in.astype(BF16), w_out.astype(BF16)
    w_gate_b, w_up_b, w_down_b = w_gate.astype(BF16), w_up.astype(BF16), w_down.astype(BF16)
    zero_state = jnp.zeros((batch, A_HEADS, LANES, LANES), F32)

    def mixers(z, s_fwd, s_bwd, l, with_output=True):
        o_b, sf_b = _hgrn2_scan(z, 0, a_blk, 3 * a_blk, lbs[1, l], s_bwd, reverse=True)
        if not with_output:
            _, sf_f = _hgrn2_scan(z, 0, a_blk, 2 * a_blk, lbs[0, l], s_fwd, reverse=False)
            return None, sf_f, sf_b
        o_a, sf_f = _hgrn2_scan(z, 0, a_blk, 2 * a_blk, lbs[0, l], s_fwd, reverse=False,
                                final=(4 * a_blk, o_b, a_norm_g[l]))
        y_f = _fourier_mixer(z, 5 * a_blk)
        cu = (5 * d_a + d_f) // d_c
        y_c = _chunk_mlp_mixer(z, cu, cu + 1, sg_norm_g[l], sg_norm_b[l], w_spatial[l], b_spatial[l])
        return (o_a, y_f, y_c), sf_f, sf_b

    def rest_of_layer(xs, ys, m, l, width):
        sh2, sc2 = m[3], m[4]
        x1, h2 = _out_projection(*ys, w_out_b, l, xs, m[2], ln1_g[l], ln1_b[l], sh2, sc2, alpha)
        hid = _ffn_up(h2, w_gate_b, w_up_b, conv_w, conv_b, l, width)
        return _ffn_down(hid, w_down_b, l, x1, m[5], ln2_g[l], ln2_b[l], alpha)

    for l in range(depth):
        last = l == depth - 1
        m_all = [mods[l, :, i * d:(i + 1) * d] for i in range(6)]
        m_x = [t[:batch].reshape(batch, 1, d) for t in m_all]
        m_c = [t[batch:batch + 1].reshape(1, 1, d) for t in m_all]

        zc = _in_projection(ctx, m_c[0], m_c[1], w_in_b, l)
        ys_c, s_f, s_b = mixers(zc, zero_state, zero_state, l, with_output=not last)
        zx = _in_projection(x, m_x[0], m_x[1], w_in_b, l)
        ys_x, _, _ = mixers(zx, s_f, s_b, l)
        x = rest_of_layer(x, ys_x, m_x, l, GRID_W)
        if not last:
            ctx = rest_of_layer(ctx, ys_c, m_c, l, ctx.shape[1])
    return x
```

```python
import functools
import math

import numpy as np
import jax
import jax.numpy as jnp
from jax import lax
from jax.experimental import pallas as pl
from jax.experimental.pallas import tpu as pltpu

F32 = jnp.float32
BF16 = jnp.bfloat16

LANES = 128
SUBLANES = 8
VMEM_LIMIT_BYTES = 56 * 1024 * 1024

A_HEADS = 8
F_GROUPS = 4
C_GROUPS = 4
MLP_CHUNK = 128
SCAN_BLOCK = 128
GRID_W = 64
LN_EPS = 1e-6
RMS_EPS = 1e-6


def _cparams(*sem):
    return pltpu.CompilerParams(dimension_semantics=sem, vmem_limit_bytes=VMEM_LIMIT_BYTES)


def _silu(x):
    return x * jax.nn.sigmoid(x)


def _silu_tanh(x):
    hx = 0.5 * x
    return hx + hx * jnp.tanh(hx)


def _layer_norm(x):
    mu = jnp.mean(x, axis=-1, keepdims=True)
    xc = x - mu
    var = jnp.mean(xc * xc, axis=-1, keepdims=True)
    return xc * lax.rsqrt(var + LN_EPS)


def _pick_tile(n, cap, quantum=LANES):
    return max(t for t in range(quantum, cap + 1, quantum) if n % t == 0)


def _cond_spec(arr):
    d = arr.shape[-1]
    if arr.shape[0] == 1:
        return pl.BlockSpec((1, 1, d), lambda b, *_: (0, 0, 0))
    return pl.BlockSpec((1, 1, d), lambda b, *_: (b, 0, 0))


def _ada_kernel(cond_ref, w_ref, b_ref, o_ref):
    s = _silu(cond_ref[...]).astype(BF16)
    o_ref[0] = jnp.dot(s, w_ref[0].astype(BF16), preferred_element_type=F32) + b_ref[0]


def _ada_modulation(cond, w_ada, b_ada):
    depth, d, n = w_ada.shape
    rows = cond.shape[0]
    tn = 1024
    return pl.pallas_call(
        _ada_kernel,
        grid=(depth, n // tn),
        in_specs=[
            pl.BlockSpec((rows, d), lambda l, j: (0, 0)),
            pl.BlockSpec((1, d, tn), lambda l, j: (l, 0, j)),
            pl.BlockSpec((1, 1, tn), lambda l, j: (l, 0, j)),
        ],
        out_specs=pl.BlockSpec((1, rows, tn), lambda l, j: (l, 0, j)),
        out_shape=jax.ShapeDtypeStruct((depth, rows, n), F32),
        compiler_params=_cparams("parallel", "parallel"),
        name="ada_modulation",
    )(cond, w_ada, b_ada.reshape(depth, 1, n))


def _inproj_kernel(x_ref, sh_ref, sc_ref, w_ref, o_ref, h_scr):
    @pl.when(pl.program_id(2) == 0)
    def _():
        h = _layer_norm(x_ref[0]) * (1.0 + sc_ref[0]) + sh_ref[0]
        h_scr[...] = h.astype(BF16)

    o_ref[0] = jnp.dot(h_scr[...], w_ref[...], preferred_element_type=F32)


def _in_projection(x, shift, scale, w, layer):
    b, l, d = x.shape
    n = w.shape[2]
    tm = min(l, 1024)
    tn = _pick_tile(n, 1792)
    return pl.pallas_call(
        _inproj_kernel,
        grid=(b, l // tm, n // tn),
        in_specs=[
            pl.BlockSpec((1, tm, d), lambda b, i, j: (b, i, 0)),
            _cond_spec(shift),
            _cond_spec(scale),
            pl.BlockSpec((None, d, tn), lambda b, i, j: (layer, 0, j)),
        ],
        out_specs=pl.BlockSpec((1, tm, tn), lambda b, i, j: (b, i, j)),
        out_shape=jax.ShapeDtypeStruct((b, l, n), F32),
        scratch_shapes=[pltpu.VMEM((tm, d), BF16)],
        compiler_params=_cparams("parallel", "parallel", "arbitrary"),
        name="in_projection",
    )(x, shift, scale, w)


LOG2E = 1.4426950408889634
SCAN_HEADS_PER_STEP = 8
SCAN_STAGE_LAGS = (3, 2)
_HIGH_LEVELS = (8, 16, 32, 64)
_M_DIAG, _M_G1, _M_G2, _M_G4, _M_Q1, _M_K1, _M_Q2, _M_Q4, _M_H8, _M_H16, _M_H32 = range(11)
_R_Q1, _R_K1, _R_Q2, _R_Q4 = range(4)


def _scan_constants(reverse):
    nb = SCAN_BLOCK
    row = np.arange(nb)[:, None]
    col = np.arange(nb)[None, :]
    x = row ^ col
    tabs = np.zeros((11, nb, nb), np.float32)
    tabs[_M_DIAG] = x == 0
    tabs[_M_G1], tabs[_M_G2], tabs[_M_G4] = x < 2, x < 4, x < 8

    def q_role(m):
        upper = (row & m) == 0
        return np.broadcast_to(upper if reverse else ~upper, (nb, nb))

    tabs[_M_Q1], tabs[_M_Q2], tabs[_M_Q4] = q_role(1), q_role(2), q_role(4)
    tabs[_M_K1] = 1.0 - tabs[_M_Q1]
    for idx, m in ((_M_H8, 8), (_M_H16, 16), (_M_H32, 32)):
        grp = np.arange(nb // 2)[:, None] // m
        tabs[idx, :nb // 2] = (col // (2 * m)) == grp
    tri = (col >= row) if reverse else (col <= row)
    roles = jnp.asarray(tabs[[_M_Q1, _M_K1, _M_Q2, _M_Q4]], BF16)
    return jnp.asarray(tabs), roles, jnp.asarray(np.concatenate([tri, tri], axis=1), BF16)


def _dot_nt(a, b):
    return lax.dot_general(a, b, (((1,), (1,)), ((), ())), preferred_element_type=F32)


def _dot_tn(a, b):
    return lax.dot_general(a, b, (((0,), (0,)), ((), ())), preferred_element_type=F32)


def _scan_gates(zq, zf, lb, c_slot, tri2):
    q = _silu_tanh(zq)
    l2_lb = jnp.log2(lb)
    e = jnp.exp2(-LOG2E * jnp.abs(zf))
    a2 = jnp.minimum(zf, 0.0) * LOG2E - jnp.log2(1.0 + e) + jnp.log1p(-lb) * LOG2E
    g2 = jnp.maximum(a2, l2_lb) + jnp.log2(1.0 + jnp.exp2(-jnp.abs(a2 - l2_lb)))
    f = jnp.exp2(g2)
    k = 1.0 - f
    g_hi = g2.astype(BF16)
    g_lo = (g2 - g_hi.astype(F32)).astype(BF16)
    c2 = jnp.dot(tri2, jnp.concatenate([g_hi, g_lo], axis=0), preferred_element_type=F32)
    c_slot[...] = c2
    return q, k, f, c2


def _scan_scores(q, k, f, c2, c_slot, tab_ref, role_ref, *, reverse):
    nb = SCAN_BLOCK
    half = nb // 2
    n_tiles = nb // SUBLANES
    tile = lambda x, i: x[i * SUBLANES:(i + 1) * SUBLANES]
    qb, kb = q.astype(BF16), k.astype(BF16)

    def crow(r):
        return c_slot[r:r + 1, :]

    lvl = _dot_nt(qb, kb) * tab_ref[_M_DIAG]
    qt = qb * (f.astype(BF16) * role_ref[_R_Q1])
    kt = kb * role_ref[_R_K1]
    lvl = lvl + _dot_nt(qt, kt) * tab_ref[_M_G1]

    upper_half_tile = lax.broadcasted_iota(jnp.int32, (SUBLANES, LANES), 0) >= SUBLANES // 2
    for m, rq, mg in ((2, _R_Q2, _M_G2), (4, _R_Q4, _M_G4)):
        refs = []
        for t in range(n_tiles):
            r0 = t * SUBLANES + (m if reverse else m - 1)
            rb = jnp.broadcast_to(crow(r0), (SUBLANES, LANES))
            if m == 2:
                rb = jnp.where(upper_half_tile, jnp.broadcast_to(crow(r0 + 4), (SUBLANES, LANES)), rb)
            refs.append(rb)
        w = jnp.exp2(-jnp.abs(c2 - jnp.concatenate(refs, axis=0))).astype(BF16)
        wq = w * role_ref[rq]
        wk = w - wq
        lvl = lvl + _dot_nt(qb * wq, kb * wk) * tab_ref[mg]
    score_tiles = [tile(lvl, i) for i in range(n_tiles)]

    for m, mh in zip(_HIGH_LEVELS, (_M_H8, _M_H16, _M_H32, None)):
        packed = m % (2 * SUBLANES) == 0
        q_parts, k_parts, q_tiles = [], [], []
        for g in range(nb // (2 * m)):
            base = g * 2 * m
            first, second = slice(base, base + m), slice(base + m, base + 2 * m)
            q_rows, k_rows = (first, second) if reverse else (second, first)
            c_ref = crow(base + m if reverse else base + m - 1)
            wq = jnp.exp2(c2[q_rows] - c_ref)
            wk = jnp.exp2(c_ref - c2[k_rows])
            if packed:
                q_parts.append(qb[q_rows] * wq.astype(BF16))
                k_part = kb[k_rows] * wk.astype(BF16)
                zeros = jnp.zeros((m, LANES), BF16)
            else:
                q_parts.append(q[q_rows] * wq)
                k_part = k[k_rows] * wk
                zeros = jnp.zeros((m, LANES), F32)
            k_parts += [zeros, k_part] if reverse else [k_part, zeros]
            q_tiles += list(range(q_rows.start // SUBLANES, q_rows.stop // SUBLANES))
        res = _dot_nt(jnp.concatenate(q_parts, axis=0).astype(BF16),
                      jnp.concatenate(k_parts, axis=0).astype(BF16))
        if mh is not None:
            res = res * tab_ref[mh, 0:half, :]
        for j, t in enumerate(q_tiles):
            score_tiles[t] = score_tiles[t] + tile(res, j)
    return jnp.concatenate(score_tiles, axis=0), qb, kb


def _scan_kernel(*refs, reverse, finalize, n_sub):
    if finalize:
        (zq_ref, zv_ref, zf_ref, lb_ref, s0_ref, tab_ref, role_ref, tri_ref, zg_ref, oo_ref, ng_ref,
         o_ref, sfin_ref, s_scr, c_scr) = refs
    else:
        (zq_ref, zv_ref, zf_ref, lb_ref, s0_ref, tab_ref, role_ref, tri_ref,
         o_ref, sfin_ref, s_scr, c_scr) = refs
    nb = SCAN_BLOCK
    hps = SCAN_HEADS_PER_STEP
    ci = pl.program_id(2)

    @pl.when(ci == 0)
    def _():
        s_scr[...] = s0_ref[0]

    tri2 = tri_ref[...]
    states = [s_scr[hh] for hh in range(hps)]
    order = [(n_sub - 1 - i) if reverse else i for i in range(n_sub)]
    items = [(j, hh) for j in order for hh in range(hps)]
    where = {it: (slice(it[0] * nb, (it[0] + 1) * nb), slice(it[1] * LANES, (it[1] + 1) * LANES)) for it in items}
    slot = {it: c_scr.at[it[0] * hps + it[1]] for it in items}
    act, scores, packed = {}, {}, {}

    def stage_gates(it):
        rows, cols = where[it]
        act[it] = _scan_gates(zq_ref[0, rows, cols], zf_ref[0, rows, cols], lb_ref[0, :, cols], slot[it], tri2)

    def stage_scores(it):
        q, k, f, c2 = act[it]
        scores[it], *packed[it] = _scan_scores(q, k, f, c2, slot[it], tab_ref, role_ref, reverse=reverse)

    def stage_output(it):
        rows, cols = where[it]
        hh = it[1]
        _, _, _, c2 = act.pop(it)
        qb, kb = packed.pop(it)
        v = zv_ref[0, rows, cols].astype(BF16)
        c_tot = slot[it][0:1, :] if reverse else slot[it][nb - 1:nb, :]
        o = jnp.dot(scores.pop(it).astype(BF16), v, preferred_element_type=F32)
        o = o + _dot_nt(qb * jnp.exp2(c2).astype(BF16), states[hh].astype(BF16))
        states[hh] = states[hh] * jnp.exp2(c_tot) + _dot_tn(v, kb * jnp.exp2(c_tot - c2).astype(BF16))
        if finalize:
            o = o + oo_ref[0, rows, cols]
            ms = jnp.mean(o * o, axis=-1, keepdims=True)
            o = o * lax.rsqrt(ms + RMS_EPS) * ng_ref[0, :, cols] * _silu_tanh(zg_ref[0, rows, cols])
        o_ref[0, rows, cols] = o.astype(o_ref.dtype)

    lag_s, lag_o = SCAN_STAGE_LAGS
    for n in range(len(items) + lag_s + lag_o):
        if n < len(items):
            stage_gates(items[n])
        if 0 <= n - lag_s < len(items):
            stage_scores(items[n - lag_s])
        if 0 <= n - lag_s - lag_o < len(items):
            stage_output(items[n - lag_s - lag_o])
    for hh in range(hps):
        s_scr[hh] = states[hh]

    @pl.when(ci == pl.num_programs(2) - 1)
    def _():
        sfin_ref[0] = s_scr[...]


def _hgrn2_scan(z, col_q, col_v, col_f, lb, s0, *, reverse, final=None):
    b, l, _ = z.shape
    hps = SCAN_HEADS_PER_STEP
    heads, hd, wd = A_HEADS, LANES, SCAN_HEADS_PER_STEP * LANES
    cb = min(l, 512)
    nc = l // cb
    n_sub = cb // SCAN_BLOCK
    assert heads % hps == 0 and col_q % hps == 0 and col_v % hps == 0 and col_f % hps == 0

    def chunk(i):
        return (nc - 1 - i) if reverse else i

    def zspec(col0):
        return pl.BlockSpec((1, cb, wd), lambda b, h, i: (b, chunk(i), col0 // hps + h))

    def whole(a):
        return pl.BlockSpec(a.shape, lambda b, h, i: (0,) * a.ndim)

    head_row = pl.BlockSpec((1, 1, wd), lambda b, h, i: (h, 0, 0))
    state = pl.BlockSpec((1, hps, hd, hd), lambda b, h, i: (b, h, 0, 0))
    tabs, roles, tri = _scan_constants(reverse)
    in_specs = [zspec(col_q), zspec(col_v), zspec(col_f), head_row, state, whole(tabs), whole(roles), whole(tri)]
    args = [z, z, z, lb.reshape(heads // hps, 1, wd), s0, tabs, roles, tri]
    if final is not None:
        col_g, other, norm_g = final
        assert col_g % hps == 0
        in_specs += [zspec(col_g), zspec(0), head_row]
        args += [z, other, norm_g.reshape(heads // hps, 1, wd)]
    out_dtype = BF16 if final is not None else F32
    return pl.pallas_call(
        functools.partial(_scan_kernel, reverse=reverse, finalize=final is not None, n_sub=n_sub),
        grid=(b, heads // hps, nc),
        in_specs=in_specs,
        out_specs=[zspec(0), state],
        out_shape=[jax.ShapeDtypeStruct((b, l, heads * hd), out_dtype),
                   jax.ShapeDtypeStruct((b, heads, hd, hd), F32)],
        scratch_shapes=[pltpu.VMEM((hps, hd, hd), F32), pltpu.VMEM((n_sub * hps, SCAN_BLOCK, hd), F32)],
        compiler_params=_cparams("parallel", "parallel", "arbitrary"),
        name="hgrn2_scan_bwd" if reverse else "hgrn2_scan_fwd",
    )(*args)


def _dft_cos_sin(n):
    idx = np.arange(n, dtype=np.int64)
    ang = 2.0 * np.pi * ((idx[:, None] * idx[None, :]) % n).astype(np.float64) / n
    return np.cos(ang), np.sin(ang)


FOURIER_PITCH = LANES + SUBLANES


def _fourier_two_stage_kernel(z_ref, csc_ref, csq_ref, csr_ref, tw1_ref, o_ref, tre_scr, tim_scr, *, q):
    nr = LANES
    pitch = FOURIER_PITCH
    csc = csc_ref[...]
    csq = csq_ref[...]
    tw_c1 = tw1_ref[0]
    tw_s1 = tw1_ref[1]

    def stage_a(r, tw):
        tw_c, tw_s = tw
        x = z_ref[0, pl.ds(r, q, stride=nr), :]
        uv = jnp.dot(x.astype(BF16), csc, preferred_element_type=F32)
        res = jnp.dot(csq, uv.astype(BF16), preferred_element_type=F32)
        cu, cv = res[:q, :LANES], res[:q, LANES:]
        su, sv = res[q:, :LANES], res[q:, LANES:]
        g_re = cu - sv
        g_im = -(cv + su)
        tre_scr[pl.ds(r, q, stride=pitch), :] = g_re * tw_c + g_im * tw_s
        tim_scr[pl.ds(r, q, stride=pitch), :] = g_im * tw_c - g_re * tw_s
        return tw_c * tw_c1 - tw_s * tw_s1, tw_s * tw_c1 + tw_c * tw_s1

    lax.fori_loop(0, nr, stage_a, (jnp.ones((q, LANES), F32), jnp.zeros((q, LANES), F32)), unroll=8)

    csr = csr_ref[...]

    def stage_b(ka, carry):
        off = pl.multiple_of(ka * pitch, SUBLANES)
        t = jnp.concatenate([tre_scr[pl.ds(off, nr), :], tim_scr[pl.ds(off, nr), :]], axis=0)
        y = jnp.dot(csr, t.astype(BF16), preferred_element_type=F32)
        o_ref[0, pl.ds(ka, nr, stride=q), :] = y
        return carry

    lax.fori_loop(0, q, stage_b, 0, unroll=8)


def _fourier_direct_kernel(z_ref, csc_ref, csl_ref, o_ref):
    uv = jnp.dot(z_ref[0].astype(BF16), csc_ref[...], preferred_element_type=F32)
    t = jnp.concatenate([uv[:, :LANES], uv[:, LANES:]], axis=0)
    o_ref[0] = jnp.dot(csl_ref[...], t.astype(BF16), preferred_element_type=F32)


def _fourier_mixer(z, col0):
    b, l, _ = z.shape
    scale = 1.0 / math.sqrt(l * LANES)
    cc, sc = _dft_cos_sin(LANES)
    csc = jnp.asarray(np.concatenate([cc, sc], axis=1) * scale, BF16)
    out_shape = jax.ShapeDtypeStruct((b, l, F_GROUPS * LANES), F32)
    slab = pl.BlockSpec((1, l, LANES), lambda b, g: (b, 0, col0 + g))
    out_slab = pl.BlockSpec((1, l, LANES), lambda b, g: (b, 0, g))

    def whole(a):
        return pl.BlockSpec(a.shape, lambda b, g: (0,) * a.ndim)

    if l <= 1024:
        cl, sl = _dft_cos_sin(l)
        csl = jnp.asarray(np.concatenate([cl, -sl], axis=1), BF16)
        return pl.pallas_call(
            _fourier_direct_kernel,
            grid=(b, F_GROUPS),
            in_specs=[slab, whole(csc), whole(csl)],
            out_specs=out_slab,
            out_shape=out_shape,
            compiler_params=_cparams("parallel", "parallel"),
            name="fourier_direct",
        )(z, csc, csl)

    q = l // LANES
    cq, sq = _dft_cos_sin(q)
    csq = jnp.asarray(np.concatenate([cq, sq], axis=0), BF16)
    csr = jnp.asarray(np.concatenate([cc, sc], axis=1), BF16)
    ang1 = 2.0 * np.pi * np.arange(q, dtype=np.float64) / l
    tw1 = jnp.asarray(np.stack([np.broadcast_to(np.cos(ang1)[:, None], (q, LANES)),
                                np.broadcast_to(np.sin(ang1)[:, None], (q, LANES))]), F32)
    return pl.pallas_call(
        functools.partial(_fourier_two_stage_kernel, q=q),
        grid=(b, F_GROUPS),
        in_specs=[slab, whole(csc), whole(csq), whole(csr), whole(tw1)],
        out_specs=out_slab,
        out_shape=out_shape,
        scratch_shapes=[pltpu.VMEM((q * FOURIER_PITCH, LANES), F32), pltpu.VMEM((q * FOURIER_PITCH, LANES), F32)],
        compiler_params=_cparams("parallel", "parallel"),
        name="fourier_two_stage",
    )(z, csc, csq, csr, tw1)


def _gelu(x):
    return jax.nn.gelu(x, approximate=True)


def _chunk_mlp_kernel(zu_ref, zv_ref, g_ref, b_ref, ws_ref, bs_ref, o_ref, *, n_chunks):
    v = _layer_norm(_gelu(zv_ref[0])) * g_ref[...] + b_ref[...]
    vb = v.astype(BF16)
    for c in range(n_chunks):
        rows = slice(c * MLP_CHUNK, (c + 1) * MLP_CHUNK)
        for g in range(C_GROUPS):
            cols = slice(g * LANES, (g + 1) * LANES)
            sv = jnp.dot(ws_ref[g], vb[rows, cols], preferred_element_type=F32) + bs_ref[g]
            o_ref[0, rows, cols] = (_gelu(zu_ref[0, rows, cols]) * sv).astype(o_ref.dtype)


def _chunk_mlp_mixer(z, col_u, col_v, ln_g, ln_b, w_s, b_s):
    b, l, _ = z.shape
    dc = C_GROUPS * LANES
    tm = min(l, 1024)
    bs_full = jnp.broadcast_to(b_s[:, :, None], (C_GROUPS, MLP_CHUNK, LANES)).astype(F32)

    def whole(a):
        return pl.BlockSpec(a.shape, lambda b, i: (0,) * a.ndim)

    ws = w_s.astype(BF16)
    g2, b2 = ln_g.reshape(1, dc), ln_b.reshape(1, dc)
    return pl.pallas_call(
        functools.partial(_chunk_mlp_kernel, n_chunks=tm // MLP_CHUNK),
        grid=(b, l // tm),
        in_specs=[pl.BlockSpec((1, tm, dc), lambda b, i: (b, i, col_u)),
                  pl.BlockSpec((1, tm, dc), lambda b, i: (b, i, col_v)),
                  whole(g2), whole(b2), whole(ws), whole(bs_full)],
        out_specs=pl.BlockSpec((1, tm, dc), lambda b, i: (b, i, 0)),
        out_shape=jax.ShapeDtypeStruct((b, l, dc), BF16),
        compiler_params=_cparams("parallel", "parallel"),
        name="chunk_mlp_mixer",
    )(z, z, g2, b2, ws, bs_full)


OUTPROJ_SUBTILE = 128


def _outproj_kernel(oa_ref, yf_ref, yc_ref, w_ref, x_ref, gate_ref, lng_ref, lnb_ref,
                    sh_ref, sc_ref, x1_ref, h_ref, *, alpha):
    tm = x_ref.shape[1]
    sub = min(tm, OUTPROJ_SUBTILE)
    for r0 in range(0, tm, sub):
        rows = slice(r0, r0 + sub)
        mixed = jnp.concatenate([oa_ref[0, rows, :], yf_ref[0, rows, :].astype(BF16), yc_ref[0, rows, :]], axis=1)
        y = jnp.dot(mixed, w_ref[...], preferred_element_type=F32)
        x1 = _layer_norm(alpha * x_ref[0, rows, :] + gate_ref[0] * y) * lng_ref[...] + lnb_ref[...]
        x1_ref[0, rows, :] = x1
        h_ref[0, rows, :] = (_layer_norm(x1) * (1.0 + sc_ref[0]) + sh_ref[0]).astype(BF16)


def _out_projection(o_a, y_f, y_c, w_out, layer, x, gate, ln_g, ln_b, shift2, scale2, alpha):
    b, l, d = x.shape
    da, df, dc = o_a.shape[-1], y_f.shape[-1], y_c.shape[-1]
    tm = min(l, 512)
    row = lambda w: pl.BlockSpec((1, tm, w), lambda b, i: (b, i, 0))

    wspec = pl.BlockSpec((None, da + df + dc, d), lambda b, i: (layer, 0, 0), pipeline_mode=pl.Buffered(1))
    vec = pl.BlockSpec((1, d), lambda b, i: (0, 0))
    return pl.pallas_call(
        functools.partial(_outproj_kernel, alpha=alpha),
        grid=(b, l // tm),
        in_specs=[row(da), row(df), row(dc),
                  wspec,
                  row(d), _cond_spec(gate), vec, vec, _cond_spec(shift2), _cond_spec(scale2)],
        out_specs=[row(d), row(d)],
        out_shape=[jax.ShapeDtypeStruct((b, l, d), F32), jax.ShapeDtypeStruct((b, l, d), BF16)],
        compiler_params=_cparams("parallel", "parallel"),
        name="out_projection",
    )(o_a, y_f, y_c, w_out, x, gate, ln_g.reshape(1, d), ln_b.reshape(1, d), shift2, scale2)


FFN_SUBTILE = 256


def _ffn_up_kernel(*refs, width, vertical, tm):
    if vertical:
        h_ref, hp_ref, hn_ref, wg_ref, wu_ref, cw_ref, cb_ref, o_ref, g_scr = refs
    else:
        h_ref, wg_ref, wu_ref, cw_ref, cb_ref, o_ref, g_scr = refs
    i = pl.program_id(1)
    h = h_ref[0]
    halo = width if vertical else 0
    n_ext = tm + 2 * halo
    tf = o_ref.shape[2]
    sub = min(tf, FFN_SUBTILE)
    subtiles = [slice(c0, c0 + sub) for c0 in range(0, tf, sub)]
    for cols in subtiles:
        wg = wg_ref[:, cols]
        g_scr[halo:halo + tm, cols] = jnp.dot(h, wg, preferred_element_type=F32)
        if vertical:
            not_first = (i > 0).astype(F32)
            not_last = (i < pl.num_programs(1) - 1).astype(F32)
            g_scr[0:halo, cols] = jnp.dot(hp_ref[0], wg, preferred_element_type=F32) * not_first
            g_scr[halo + tm:, cols] = jnp.dot(hn_ref[0], wg, preferred_element_type=F32) * not_last
    n_grid_rows = n_ext // width
    sublane = lax.broadcasted_iota(jnp.int32, (n_grid_rows, SUBLANES, sub), 1)

    def zero_column(x, first):
        x3 = x.reshape(n_grid_rows, width, sub)
        if first:
            edge = jnp.where(sublane == 0, 0.0, x3[:, :SUBLANES, :])
            x3 = jnp.concatenate([edge, x3[:, SUBLANES:, :]], axis=1)
        else:
            edge = jnp.where(sublane == SUBLANES - 1, 0.0, x3[:, width - SUBLANES:, :])
            x3 = jnp.concatenate([x3[:, :width - SUBLANES, :], edge], axis=1)
        return x3.reshape(n_ext, sub)

    for cols in subtiles:
        ext = g_scr[:, cols]
        left = zero_column(pltpu.roll(ext, 1, axis=0), True)
        right = zero_column(pltpu.roll(ext, n_ext - 1, axis=0), False)
        taps = (left, ext, right)
        acc = None
        for dh in (range(3) if vertical else (1,)):
            base = halo + (dh - 1) * width
            for dw in range(3):
                term = taps[dw][base:base + tm, :] * cw_ref[dh * 3 + dw:dh * 3 + dw + 1, cols]
                acc = term if acc is None else acc + term
        gate = _silu_tanh(acc + cb_ref[:, cols])
        up = jnp.dot(h, wu_ref[:, cols], preferred_element_type=F32)
        o_ref[0, :, cols] = (gate * up).astype(o_ref.dtype)


def _ffn_up(h, w_gate, w_up, conv_w, conv_b, layer, width):
    b, l, d = h.shape
    depth, _, f = w_gate.shape
    vertical = l > width
    tm = min(l, 1024)
    tf = 512
    assert tm % width == 0 and f % tf == 0
    rows_per_tile = tm // width
    n_rows = l // width
    in_specs = [pl.BlockSpec((1, tm, d), lambda b, i, j: (b, i, 0))]
    args = [h]
    if vertical:
        in_specs += [
            pl.BlockSpec((1, width, d), lambda b, i, j: (b, jnp.maximum(i * rows_per_tile - 1, 0), 0)),
            pl.BlockSpec((1, width, d), lambda b, i, j: (b, jnp.minimum((i + 1) * rows_per_tile, n_rows - 1), 0)),
        ]
        args += [h, h]
    wspec = pl.BlockSpec((None, d, tf), lambda b, i, j: (layer, 0, j))
    in_specs += [wspec, wspec,
                 pl.BlockSpec((None, 9, tf), lambda b, i, j: (layer, 0, j)),
                 pl.BlockSpec((None, 1, tf), lambda b, i, j: (layer, 0, j))]
    args += [w_gate, w_up, conv_w.reshape(depth, 9, f), conv_b.reshape(depth, 1, f)]
    ext_rows = tm + (2 * width if vertical else 0)
    return pl.pallas_call(
        functools.partial(_ffn_up_kernel, width=width, vertical=vertical, tm=tm),
        grid=(b, l // tm, f // tf),
        in_specs=in_specs,
        out_specs=pl.BlockSpec((1, tm, tf), lambda b, i, j: (b, i, j)),
        out_shape=jax.ShapeDtypeStruct((b, l, f), BF16),
        scratch_shapes=[pltpu.VMEM((ext_rows, tf), F32)],
        compiler_params=_cparams("parallel", "parallel", "arbitrary"),
        name="ffn_up",
    )(*args)


def _ffn_down_kernel(hid_ref, w_ref, x_ref, gate_ref, lng_ref, lnb_ref, o_ref, *, alpha):
    f = jnp.dot(hid_ref[0], w_ref[...], preferred_element_type=F32)
    y = alpha * x_ref[0] + gate_ref[0] * f
    o_ref[0] = _layer_norm(y) * lng_ref[...] + lnb_ref[...]


def _ffn_down(hid, w_down, layer, x, gate, ln_g, ln_b, alpha):
    b, l, f = hid.shape
    d = x.shape[-1]
    tm = min(l, 512)
    vec = pl.BlockSpec((1, d), lambda b, i: (0, 0))
    return pl.pallas_call(
        functools.partial(_ffn_down_kernel, alpha=alpha),
        grid=(b, l // tm),
        in_specs=[pl.BlockSpec((1, tm, f), lambda b, i: (b, i, 0)),
                  pl.BlockSpec((None, f, d), lambda b, i: (layer, 0, 0), pipeline_mode=pl.Buffered(1)),
                  pl.BlockSpec((1, tm, d), lambda b, i: (b, i, 0)),
                  _cond_spec(gate), vec, vec],
        out_specs=pl.BlockSpec((1, tm, d), lambda b, i: (b, i, 0)),
        out_shape=jax.ShapeDtypeStruct((b, l, d), F32),
        compiler_params=_cparams("parallel", "parallel"),
        name="ffn_down",
    )(hid, w_down, x, gate, ln_g.reshape(1, d), ln_b.reshape(1, d))


def kernel(x, c, ctx, c_ctx, w_ada, b_ada, w_in, lower_bounds, a_norm_g, sg_norm_g, sg_norm_b,
           w_spatial, b_spatial, w_out, ln1_g, ln1_b, w_gate, w_up, conv_w, conv_b, w_down, ln2_g, ln2_b):
    batch, seq, d = x.shape
    depth = w_in.shape[0]
    d_a = a_norm_g.shape[-1]
    d_c = sg_norm_g.shape[-1]
    d_f = w_in.shape[-1] - 5 * d_a - 2 * d_c
    assert d_a == A_HEADS * LANES and d_c == C_GROUPS * LANES and d_f == F_GROUPS * LANES
    alpha = (2 * depth) ** 0.25
    a_blk = d_a // LANES

    lbs = jnp.cumsum(jax.nn.softmax(lower_bounds.astype(F32), axis=1), axis=1)
    lbs = lbs - lbs[:, :1]

    rows = -(-(batch + 1) // SUBLANES) * SUBLANES
    cond = jnp.zeros((rows, d), F32).at[:batch].set(c).at[batch].set(c_ctx)
    mods = _ada_modulation(cond, w_ada, b_ada)

    w_in_b, w_out_b = w_in.astype(BF16), w_out.astype(BF16)
    w_gate_b, w_up_b, w_down_b = w_gate.astype(BF16), w_up.astype(BF16), w_down.astype(BF16)
    zero_state = jnp.zeros((batch, A_HEADS, LANES, LANES), F32)

    def mixers(z, s_fwd, s_bwd, l, with_output=True):
        o_b, sf_b = _hgrn2_scan(z, 0, a_blk, 3 * a_blk, lbs[1, l], s_bwd, reverse=True)
        if not with_output:
            _, sf_f = _hgrn2_scan(z, 0, a_blk, 2 * a_blk, lbs[0, l], s_fwd, reverse=False)
            return None, sf_f, sf_b
        o_a, sf_f = _hgrn2_scan(z, 0, a_blk, 2 * a_blk, lbs[0, l], s_fwd, reverse=False,
                                final=(4 * a_blk, o_b, a_norm_g[l]))
        y_f = _fourier_mixer(z, 5 * a_blk)
        cu = (5 * d_a + d_f) // d_c
        y_c = _chunk_mlp_mixer(z, cu, cu + 1, sg_norm_g[l], sg_norm_b[l], w_spatial[l], b_spatial[l])
        return (o_a, y_f, y_c), sf_f, sf_b

    def rest_of_layer(xs, ys, m, l, width):
        sh2, sc2 = m[3], m[4]
        x1, h2 = _out_projection(*ys, w_out_b, l, xs, m[2], ln1_g[l], ln1_b[l], sh2, sc2, alpha)
        hid = _ffn_up(h2, w_gate_b, w_up_b, conv_w, conv_b, l, width)
        return _ffn_down(hid, w_down_b, l, x1, m[5], ln2_g[l], ln2_b[l], alpha)

    for l in range(depth):
        last = l == depth - 1
        m_all = [mods[l, :, i * d:(i + 1) * d] for i in range(6)]
        m_x = [t[:batch].reshape(batch, 1, d) for t in m_all]
        m_c = [t[batch:batch + 1].reshape(1, 1, d) for t in m_all]

        zc = _in_projection(ctx, m_c[0], m_c[1], w_in_b, l)
        ys_c, s_f, s_b = mixers(zc, zero_state, zero_state, l, with_output=not last)
        zx = _in_projection(x, m_x[0], m_x[1], w_in_b, l)
        ys_x, _, _ = mixers(zx, s_f, s_b, l)
        x = rest_of_layer(x, ys_x, m_x, l, GRID_W)
        if not last:
            ctx = rest_of_layer(ctx, ys_c, m_c, l, ctx.shape[1])
    return x
```
